```python
import math
import jax, jax.numpy as jnp
from jax import lax
import numpy as np

D_MODEL = 1024
BATCH = 4
SEQ = 8192
DEPTH = 2

HEAD_DIM = 64
ROPE_THETA = 10000.0
NORM_EPS = 1e-6
Q_BLOCK = 128
BIG = 1e9

NSA_HEADS = 8
NSA_KV_GROUPS = 2
NSA_HEADS_PER_GROUP = NSA_HEADS // NSA_KV_GROUPS
CMP_BLOCK = 32
CMP_STRIDE = 16
CMP_HIDDEN = 256
SLC_BLOCK = 64
SLC_TOPK = 16
WINDOW = 512
NSA_WIDTH = NSA_HEADS * HEAD_DIM
NSA_KV_WIDTH = NSA_KV_GROUPS * HEAD_DIM

DIFF_HEADS = 4
DIFF_QK_WIDTH = DIFF_HEADS * 2 * HEAD_DIM
DIFF_V_DIM = 2 * HEAD_DIM
DIFF_WIDTH = DIFF_HEADS * DIFF_V_DIM

SGU_CHUNK = 128
SGU_GROUPS = 4
SGU_WIDTH = 512
SGU_GROUP_DIM = SGU_WIDTH // SGU_GROUPS

N_BRANCHES = 3
BRANCH_WIDTH = 512
D_FF = -(-8 * D_MODEL // (3 * 256)) * 256

IN_SPLITS = [NSA_WIDTH, 6 * NSA_KV_WIDTH, NSA_HEADS * 3,
             DIFF_QK_WIDTH, DIFF_QK_WIDTH, DIFF_WIDTH, 2 * SGU_WIDTH]
D_IN = sum(IN_SPLITS)
IN_OFFSETS = [int(v) for v in np.cumsum(IN_SPLITS)[:-1]]

kernel_name = "hybrid_nsa_diffattn_sgu_block"


def rmsnorm(x, g):
    xf = x.astype(jnp.float32)
    y = xf * lax.rsqrt(jnp.mean(xf * xf, axis=-1, keepdims=True) + NORM_EPS)
    return y.astype(x.dtype) * g


def rope(x, positions):
    half = x.shape[-1] // 2
    inv_freq = ROPE_THETA ** (-jnp.arange(half, dtype=jnp.float32) / half)
    ang = positions.astype(jnp.float32)[..., None] * inv_freq
    cos = jnp.cos(ang)[:, :, None, :].astype(x.dtype)
    sin = jnp.sin(ang)[:, :, None, :].astype(x.dtype)
    x1, x2 = x[..., :half], x[..., half:]
    return jnp.concatenate([x1 * cos - x2 * sin, x2 * cos + x1 * sin], axis=-1)


def masked_softmax(logits, mask):
    lf = jnp.where(mask, logits.astype(jnp.float32), -1e30)
    p = jax.nn.softmax(lf, axis=-1)
    return jnp.where(mask, p, 0.0)


def map_query_blocks(fn, seq):
    out = lax.map(fn, jnp.arange(seq // Q_BLOCK))
    nb, b, t, w = out.shape
    return out.transpose(1, 0, 2, 3).reshape(b, nb * t, w)


def nsa_attention(q, k_cmp, v_cmp, k_slc, v_slc, k_win, v_win, gates, positions,
                  pos_k, wk1, wk2, pos_v, wv1, wv2):
    B, S = q.shape[:2]
    G, HPG, d = NSA_KV_GROUPS, NSA_HEADS_PER_GROUP, HEAD_DIM
    scale = d ** -0.5
    q_rot = rope(q, positions)
    k_slc = rope(k_slc, positions)
    k_win = rope(k_win, positions)

    n_cmp = (S - CMP_BLOCK) // CMP_STRIDE + 1
    cmp_idx = np.arange(n_cmp)[:, None] * CMP_STRIDE + np.arange(CMP_BLOCK)[None, :]
    cmp_end = jnp.asarray(np.arange(n_cmp) * CMP_STRIDE + CMP_BLOCK - 1)

    def compress(kv, pos_emb, w1, w2):
        blocks = kv[:, cmp_idx] + pos_emb[:, None, :]
        blocks = blocks.transpose(0, 1, 3, 2, 4).reshape(B, n_cmp, G, CMP_BLOCK * d)
        return jax.nn.gelu(blocks @ w1) @ w2

    kc = compress(k_cmp, pos_k, wk1, wk2)
    vc = compress(v_cmp, pos_v, wv1, wv2)

    n_slc = S // SLC_BLOCK
    slc_k = min(SLC_TOPK, n_slc)
    c_start = np.arange(n_cmp)[:, None] * CMP_STRIDE
    s_start = np.arange(n_slc)[None, :] * SLC_BLOCK
    overlap = jnp.asarray(((c_start < s_start + SLC_BLOCK) &
                           (c_start + CMP_BLOCK > s_start)).astype(np.float32))
    slc_starts = jnp.arange(n_slc) * SLC_BLOCK
    blk_ids = jnp.arange(n_slc)

    kb = k_slc.reshape(B, n_slc, SLC_BLOCK, G, d).transpose(0, 3, 1, 2, 4)
    vb = v_slc.reshape(B, n_slc, SLC_BLOCK, G, d).transpose(0, 3, 1, 2, 4)
    gather = jax.vmap(jax.vmap(lambda blocks, ids: blocks[ids]))

    k_win_pad = jnp.pad(k_win, ((0, 0), (WINDOW, 0), (0, 0), (0, 0)))
    v_win_pad = jnp.pad(v_win, ((0, 0), (WINDOW, 0), (0, 0), (0, 0)))

    def block(i):
        s0 = i * Q_BLOCK
        t = s0 + jnp.arange(Q_BLOCK)
        qg = lax.dynamic_slice_in_dim(q, s0, Q_BLOCK, axis=1).reshape(B, Q_BLOCK, G, HPG, d)
        qrg = lax.dynamic_slice_in_dim(q_rot, s0, Q_BLOCK, axis=1).reshape(B, Q_BLOCK, G, HPG, d)
        gb = lax.dynamic_slice_in_dim(gates, s0, Q_BLOCK, axis=1).reshape(B, Q_BLOCK, G, HPG, 3)

        logit_c = jnp.einsum('btghd,bngd->bghtn', qg, kc) * scale
        p_c = masked_softmax(logit_c, cmp_end[None, :] <= t[:, None])
        o_cmp = jnp.einsum('bghtn,bngd->btghd', p_c.astype(vc.dtype), vc)

        imp = jnp.einsum('bghtn,nj->bgtj', p_c, overlap)
        eligible = slc_starts[None, :] <= t[:, None]
        cur = t // SLC_BLOCK
        forced = (blk_ids[None, :] == 0) | (blk_ids[None, :] == cur[:, None]) | \
                 (blk_ids[None, :] == cur[:, None] - 1)
        score = jnp.where(forced, BIG, jnp.where(eligible, imp, -BIG))
        _, idx = lax.top_k(score, slc_k)
        ks = gather(kb, idx).reshape(B, G, Q_BLOCK, slc_k * SLC_BLOCK, d)
        vs = gather(vb, idx).reshape(B, G, Q_BLOCK, slc_k * SLC_BLOCK, d)
        kpos = (idx[..., None] * SLC_BLOCK + jnp.arange(SLC_BLOCK)).reshape(B, G, Q_BLOCK, slc_k * SLC_BLOCK)
        mask_s = (kpos <= t[None, None, :, None])[:, :, None]
        logit_s = jnp.einsum('btghd,bgtkd->bghtk', qrg, ks) * scale
        p_s = masked_softmax(logit_s, mask_s)
        o_slc = jnp.einsum('bghtk,bgtkd->btghd', p_s.astype(vs.dtype), vs)

        kw = lax.dynamic_slice_in_dim(k_win_pad, s0, Q_BLOCK + WINDOW, axis=1)
        vw = lax.dynamic_slice_in_dim(v_win_pad, s0, Q_BLOCK + WINDOW, axis=1)
        wpos = s0 - WINDOW + jnp.arange(Q_BLOCK + WINDOW)
        mask_w = (wpos[None, :] <= t[:, None]) & (wpos[None, :] > t[:, None] - WINDOW) & (wpos[None, :] >= 0)
        logit_w = jnp.einsum('btghd,bkgd->bghtk', qrg, kw) * scale
        p_w = masked_softmax(logit_w, mask_w)
        o_win = jnp.einsum('bghtk,bkgd->btghd', p_w.astype(vw.dtype), vw)

        out = gb[..., 0:1] * o_cmp + gb[..., 1:2] * o_slc + gb[..., 2:3] * o_win
        return out.reshape(B, Q_BLOCK, NSA_WIDTH)

    return map_query_blocks(block, S)


def diff_attention(q, k, v, positions, lq1, lk1, lq2, lk2, subln_g, lambda_init):
    B, S = q.shape[:2]
    scale = HEAD_DIM ** -0.5
    q = rope(q, positions)
    k = rope(k, positions)
    lam = (jnp.exp(jnp.sum(lq1 * lk1).astype(jnp.float32))
           - jnp.exp(jnp.sum(lq2 * lk2).astype(jnp.float32)) + lambda_init)
    kpos = jnp.arange(S)

    def block(i):
        s0 = i * Q_BLOCK
        t = s0 + jnp.arange(Q_BLOCK)
        qb = lax.dynamic_slice_in_dim(q, s0, Q_BLOCK, axis=1)
        logits = jnp.einsum('bthd,bshd->bhts', qb, k) * scale
        p = masked_softmax(logits, kpos[None, :] <= t[:, None])
        p = p.reshape(B, DIFF_HEADS, 2, Q_BLOCK, S)
        a = p[:, :, 0] - lam * p[:, :, 1]
        o = jnp.einsum('bhts,bshe->bthe', a.astype(v.dtype), v)
        o = rmsnorm(o, subln_g) * (1.0 - lambda_init)
        return o.reshape(B, Q_BLOCK, DIFF_WIDTH)

    return map_query_blocks(block, S)


def chunked_sgu(uv, norm_g, w_s, b_s):
    B, S = uv.shape[:2]
    z = jax.nn.gelu(uv)
    u, v = z[..., :SGU_WIDTH], z[..., SGU_WIDTH:]
    v = rmsnorm(v, norm_g).reshape(B, S // SGU_CHUNK, SGU_CHUNK, SGU_GROUPS, SGU_GROUP_DIM)
    causal = jnp.tril(jnp.ones((SGU_CHUNK, SGU_CHUNK), dtype=bool))
    w = jnp.where(causal[None], w_s, 0.0)
    s = jnp.einsum('gts,bnsgc->bntgc', w, v) + b_s.T[:, :, None]
    return u * s.reshape(B, S, SGU_WIDTH)


def setup_inputs(seed: int = 0) -> dict:
    key = jax.random.key(seed)
    ks = iter(jax.random.split(key, 40))
    L, D, d = DEPTH, D_MODEL, HEAD_DIM

    def normal(shape, scale):
        return jax.random.normal(next(ks), shape, jnp.float32) * scale

    def gain(shape):
        return 1.0 + normal(shape, 0.05)

    x = normal((BATCH, SEQ, D), 1.0)
    offset = jax.random.randint(next(ks), (BATCH, 1), 0, 4096, dtype=jnp.int32)
    positions = offset + jnp.arange(SEQ, dtype=jnp.int32)[None, :]
    return {
        "x": x,
        "positions": positions,
        "attn_norm": gain((L, D)),
        "w_in": normal((L, D, D_IN), D ** -0.5),
        "cmp_pos_k": normal((L, CMP_BLOCK, d), 0.1),
        "cmp_k_w1": normal((L, CMP_BLOCK * d, CMP_HIDDEN), (CMP_BLOCK * d) ** -0.5),
        "cmp_k_w2": normal((L, CMP_HIDDEN, d), CMP_HIDDEN ** -0.5),
        "cmp_pos_v": normal((L, CMP_BLOCK, d), 0.1),
        "cmp_v_w1": normal((L, CMP_BLOCK * d, CMP_HIDDEN), (CMP_BLOCK * d) ** -0.5),
        "cmp_v_w2": normal((L, CMP_HIDDEN, d), CMP_HIDDEN ** -0.5),
        "diff_lq1": normal((L, d), 0.1),
        "diff_lk1": normal((L, d), 0.1),
        "diff_lq2": normal((L, d), 0.1),
        "diff_lk2": normal((L, d), 0.1),
        "diff_subln": gain((L, DIFF_V_DIM)),
        "sgu_norm": gain((L, SGU_WIDTH)),
        "sgu_w": normal((L, SGU_GROUPS, SGU_CHUNK, SGU_CHUNK), 0.5 * SGU_CHUNK ** -0.5),
        "sgu_b": 1.0 + normal((L, SGU_GROUPS, SGU_CHUNK), 0.1),
        "w_branch_a": normal((L, NSA_WIDTH, D), NSA_WIDTH ** -0.5),
        "w_branch_b": normal((L, DIFF_WIDTH, D), DIFF_WIDTH ** -0.5),
        "w_branch_c": normal((L, SGU_WIDTH, D), SGU_WIDTH ** -0.5),
        "w_merge": normal((L, D, N_BRANCHES * D), D ** -0.5),
        "b_merge": normal((L, N_BRANCHES * D), 0.02),
        "w_out": normal((L, D, D), D ** -0.5),
        "ffn_norm": gain((L, D)),
        "w_ffn1": normal((L, D, D_FF), D ** -0.5),
        "w_ffn3": normal((L, D, D_FF), D ** -0.5),
        "w_ffn2": normal((L, D_FF, D), D_FF ** -0.5),
        "final_norm": gain((D,)),
    }


def reference(x, positions, attn_norm, w_in, cmp_pos_k, cmp_k_w1, cmp_k_w2, cmp_pos_v, cmp_v_w1,
              cmp_v_w2, diff_lq1, diff_lk1, diff_lq2, diff_lk2, diff_subln, sgu_norm, sgu_w, sgu_b,
              w_branch_a, w_branch_b, w_branch_c, w_merge, b_merge, w_out, ffn_norm, w_ffn1,
              w_ffn3, w_ffn2, final_norm):
    B, S, D = x.shape
    for l in range(DEPTH):
        lambda_init = 0.8 - 0.6 * math.exp(-0.3 * l)
        h = rmsnorm(x, attn_norm[l])
        proj = h @ w_in[l]
        q_a, kv_a, g_a, q_b, k_b, v_b, uv_c = jnp.split(proj, IN_OFFSETS, axis=-1)
        k_cmp, v_cmp, k_slc, v_slc, k_win, v_win = [
            t.reshape(B, S, NSA_KV_GROUPS, HEAD_DIM) for t in jnp.split(kv_a, 6, axis=-1)]
        o_a = nsa_attention(q_a.reshape(B, S, NSA_HEADS, HEAD_DIM), k_cmp, v_cmp, k_slc, v_slc,
                            k_win, v_win, jax.nn.sigmoid(g_a).reshape(B, S, NSA_HEADS, 3), positions,
                            cmp_pos_k[l], cmp_k_w1[l], cmp_k_w2[l], cmp_pos_v[l], cmp_v_w1[l], cmp_v_w2[l])
        o_b = diff_attention(q_b.reshape(B, S, 2 * DIFF_HEADS, HEAD_DIM),
                             k_b.reshape(B, S, 2 * DIFF_HEADS, HEAD_DIM),
                             v_b.reshape(B, S, DIFF_HEADS, DIFF_V_DIM), positions,
                             diff_lq1[l], diff_lk1[l], diff_lq2[l], diff_lk2[l], diff_subln[l], lambda_init)
        o_c = chunked_sgu(uv_c, sgu_norm[l], sgu_w[l], sgu_b[l])
        gates = jax.nn.sigmoid(h @ w_merge[l] + b_merge[l]).reshape(B, S, N_BRANCHES, D)
        mixed = (gates[:, :, 0] * (o_a @ w_branch_a[l])
                 + gates[:, :, 1] * (o_b @ w_branch_b[l])
                 + gates[:, :, 2] * (o_c @ w_branch_c[l]))
        x = x + mixed @ w_out[l]
        h = rmsnorm(x, ffn_norm[l])
        x = x + (jax.nn.silu(h @ w_ffn1[l]) * (h @ w_ffn3[l])) @ w_ffn2[l]
    return rmsnorm(x, final_norm)
```

```python
import functools
import math

import jax
import jax.numpy as jnp
from jax import lax
from jax.experimental import pallas as pl
from jax.experimental.pallas import tpu as pltpu

HEAD_DIM = 64
HALF = HEAD_DIM // 2
ROPE_THETA = 10000.0
NORM_EPS = 1e-6
BIG = 1e9
NEG = -1e30
REMOVED = -3e38

NSA_HEADS = 8
NSA_GROUPS = 2
HPG = NSA_HEADS // NSA_GROUPS
CMP_BLOCK = 32
CMP_STRIDE = 16
CMP_HIDDEN = 256
SLC_BLOCK = 64
SLC_TOPK = 16
WINDOW = 512
DIFF_HEADS = 4
DIFF_V_DIM = 2 * HEAD_DIM
SGU_CHUNK = 128
SGU_GROUPS = 4
SGU_WIDTH = 512
BRANCH = 512
SCALE = HEAD_DIM ** -0.5

LANE = 128
NSP = 128
MXU_DTYPE = jnp.bfloat16
VMEM_LIMIT = 56 * 1024 * 1024

TM = 512
NSA_TQ = 128
NSA_TK = 512
DIFF_TQ = 256
DIFF_TK = 512
KCH = 128


def _params(*sem):
    return pltpu.CompilerParams(dimension_semantics=sem, vmem_limit_bytes=VMEM_LIMIT)


def _const_spec(shape):
    nd = len(shape)
    return pl.BlockSpec(shape, lambda *_: (0,) * nd, pipeline_mode=pl.Buffered(1))


def _dot(a, b):
    return jnp.dot(a, b, preferred_element_type=jnp.float32)


def _dot_nt(a, b):
    return lax.dot_general(a, b, (((1,), (1,)), ((), ())), preferred_element_type=jnp.float32)


def _rmsnorm_rows(x, g):
    return x * lax.rsqrt(jnp.mean(x * x, axis=-1, keepdims=True) + NORM_EPS) * g


def _tables_kernel(pos_col_ref, pos_row_ref, f_row_ref, f_col_ref, cos_ref, sin_ref, cost_ref, sint_ref):
    ang = pos_col_ref[...].astype(jnp.float32) * f_row_ref[...]
    lane = lax.broadcasted_iota(jnp.int32, ang.shape, 1)
    cos_ref[...] = jnp.cos(ang)
    sin_ref[...] = jnp.where((lane % HEAD_DIM) < HALF, -jnp.sin(ang), jnp.sin(ang))
    angt = f_col_ref[...] * pos_row_ref[...].astype(jnp.float32)
    cost_ref[...] = jnp.cos(angt)
    sint_ref[...] = jnp.sin(angt)


def _rope_tables(positions):
    n = positions.size
    inv_freq = ROPE_THETA ** (-jnp.arange(HALF, dtype=jnp.float32) / HALF)
    f_row = jnp.tile(inv_freq, LANE // HALF).reshape(1, LANE)
    f_col = inv_freq.reshape(HALF, 1)
    pos_col = positions.reshape(n, 1)
    pos_row = positions.reshape(1, n)
    return pl.pallas_call(
        _tables_kernel,
        grid=(n // TM,),
        in_specs=[pl.BlockSpec((TM, 1), lambda i: (i, 0)),
                  pl.BlockSpec((1, TM), lambda i: (0, i)),
                  _const_spec((1, LANE)), _const_spec((HALF, 1))],
        out_specs=[pl.BlockSpec((TM, LANE), lambda i: (i, 0)),
                   pl.BlockSpec((TM, LANE), lambda i: (i, 0)),
                   pl.BlockSpec((HALF, TM), lambda i: (0, i)),
                   pl.BlockSpec((HALF, TM), lambda i: (0, i))],
        out_shape=[jax.ShapeDtypeStruct((n, LANE), jnp.float32),
                   jax.ShapeDtypeStruct((n, LANE), jnp.float32),
                   jax.ShapeDtypeStruct((HALF, n), jnp.float32),
                   jax.ShapeDtypeStruct((HALF, n), jnp.float32)],
        compiler_params=_params("parallel"),
        name="rope_tables",
    )(pos_col, pos_row, f_row, f_col)


P_KVC, P_KSLC, P_KWIN, P_KB, P_UV, P_END = 0, 256, 384, 512, 1024, 2048
T_QA, T_VSLC, T_VWIN, T_G, T_QB, T_VB, T_END = 0, 512, 640, 768, 800, 1312, 1824
G_ROWS = T_QB - T_G


def _rope_plain(y, cos, sin):
    lane = lax.broadcasted_iota(jnp.int32, y.shape, 1)
    swapped = jnp.where((lane % HEAD_DIM) < HALF, pltpu.roll(y, LANE - HALF, 1), pltpu.roll(y, HALF, 1))
    return y * cos + swapped * sin


def _rope_t(y, cos, sin):
    y1, y2 = y[:HALF], y[HALF:]
    return jnp.concatenate([y1 * cos - y2 * sin, y2 * cos + y1 * sin], axis=0)


def _inproj_kernel(x_ref, g_ref, wp_ref, wt_ref, cos_ref, sin_ref, cost_ref, sint_ref,
                   sgu_g_ref, sgu_w_ref, sgu_b_ref,
                   kvc_ref, kslc_ref, kwin_ref, kb_ref, oc_ref,
                   qat_ref, qart_ref, vslct_ref, vwint_ref, gt_ref, qbt_ref, vbt_ref):
    x = x_ref[...]
    hb = _rmsnorm_rows(x, g_ref[...]).astype(MXU_DTYPE)
    cos, sin = cos_ref[...], sin_ref[...]
    cost, sint = cost_ref[...], sint_ref[...]
    tm = x.shape[0]

    kvc_ref[...] = _dot(hb, wp_ref[:, P_KVC:P_KSLC])
    kslc_ref[...] = _rope_plain(_dot(hb, wp_ref[:, P_KSLC:P_KWIN]), cos, sin).astype(kslc_ref.dtype)
    kwin_ref[...] = _rope_plain(_dot(hb, wp_ref[:, P_KWIN:P_KB]), cos, sin).astype(kwin_ref.dtype)
    kb = _dot(hb, wp_ref[:, P_KB:P_UV])
    for c in range((P_UV - P_KB) // LANE):
        sl = slice(c * LANE, (c + 1) * LANE)
        kb_ref[:, sl] = _rope_plain(kb[:, sl], cos, sin).astype(kb_ref.dtype)

    z = jax.nn.gelu(_dot(hb, wp_ref[:, P_UV:P_END]))
    u = z[:, :SGU_WIDTH]
    vn = _rmsnorm_rows(z[:, SGU_WIDTH:], sgu_g_ref[...]).astype(MXU_DTYPE)
    gdim = SGU_WIDTH // SGU_GROUPS
    for ci in range(tm // SGU_CHUNK):
        rows = slice(ci * SGU_CHUNK, (ci + 1) * SGU_CHUNK)
        for gi in range(SGU_GROUPS):
            cols = slice(gi * gdim, (gi + 1) * gdim)
            s = _dot(sgu_w_ref[gi], vn[rows, cols]) + sgu_b_ref[:, cols]
            oc_ref[rows, cols] = (u[rows, cols] * s).astype(oc_ref.dtype)

    qat = _dot_nt(wt_ref[T_QA:T_VSLC, :], hb) * SCALE
    qat_ref[...] = qat.astype(qat_ref.dtype)
    for h in range(NSA_HEADS):
        sl = slice(h * HEAD_DIM, (h + 1) * HEAD_DIM)
        qart_ref[sl, :] = _rope_t(qat[sl], cost, sint).astype(qart_ref.dtype)
    vst = _dot_nt(wt_ref[T_VSLC:T_VWIN, :], hb).astype(vslct_ref.dtype)
    vwt = _dot_nt(wt_ref[T_VWIN:T_G, :], hb).astype(vwint_ref.dtype)
    for c in range(tm // KCH):
        sl = slice(c * KCH, (c + 1) * KCH)
        vslct_ref[c] = vst[:, sl]
        vwint_ref[c] = vwt[:, sl]
    gt_ref[...] = jax.nn.sigmoid(_dot_nt(wt_ref[T_G:T_QB, :], hb))
    qbt = _dot_nt(wt_ref[T_QB:T_VB, :], hb) * SCALE
    for h in range(2 * DIFF_HEADS):
        sl = slice(h * HEAD_DIM, (h + 1) * HEAD_DIM)
        qbt_ref[sl, :] = _rope_t(qbt[sl], cost, sint).astype(qbt_ref.dtype)
    vbt = _dot_nt(wt_ref[T_VB:T_END, :], hb).astype(vbt_ref.dtype)
    for c in range(tm // KCH):
        vbt_ref[c] = vbt[:, c * KCH:(c + 1) * KCH]


def _inproj(x2, g, wp, wt, tables, sgu_g, sgu_w, sgu_b):
    n, d = x2.shape
    cos, sin, cost, sint = tables
    row = lambda w: pl.BlockSpec((TM, w), lambda i: (i, 0))
    colt = lambda r: pl.BlockSpec((r, TM), lambda i: (0, i))
    chunk = lambda r: pl.BlockSpec((TM // KCH, r, KCH), lambda i: (i, 0, 0))
    bf = MXU_DTYPE
    outs = [
        (row(256), jax.ShapeDtypeStruct((n, 256), jnp.float32)),
        (row(LANE), jax.ShapeDtypeStruct((n, LANE), bf)),
        (row(LANE), jax.ShapeDtypeStruct((n, LANE), bf)),
        (row(512), jax.ShapeDtypeStruct((n, 512), bf)),
        (row(SGU_WIDTH), jax.ShapeDtypeStruct((n, SGU_WIDTH), bf)),
        (colt(512), jax.ShapeDtypeStruct((512, n), bf)),
        (colt(512), jax.ShapeDtypeStruct((512, n), bf)),
        (chunk(LANE), jax.ShapeDtypeStruct((n // KCH, LANE, KCH), bf)),
        (chunk(LANE), jax.ShapeDtypeStruct((n // KCH, LANE, KCH), bf)),
        (colt(G_ROWS), jax.ShapeDtypeStruct((G_ROWS, n), jnp.float32)),
        (colt(512), jax.ShapeDtypeStruct((512, n), bf)),
        (chunk(512), jax.ShapeDtypeStruct((n // KCH, 512, KCH), bf)),
    ]
    return pl.pallas_call(
        _inproj_kernel,
        grid=(n // TM,),
        in_specs=[row(d), _const_spec((1, d)), _const_spec(wp.shape), _const_spec(wt.shape),
                  row(LANE), row(LANE), colt(HALF), colt(HALF),
                  _const_spec(sgu_g.shape), _const_spec(sgu_w.shape), _const_spec(sgu_b.shape)],
        out_specs=[o[0] for o in outs],
        out_shape=[o[1] for o in outs],
        compiler_params=_params("parallel"),
        name="inproj",
    )(x2, g, wp, wt, cos, sin, cost, sint, sgu_g, sgu_w, sgu_b)


def _compress_k_kernel(x_ref, pa_ref, pb_ref, w1a_ref, w1b_ref, w2_ref, o_ref):
    nc = x_ref.shape[1]
    acc = jnp.zeros(o_ref.shape, jnp.float32)
    for g in range(NSA_GROUPS):
        xg = x_ref[g]
        a = _dot((xg + pa_ref[...]).astype(MXU_DTYPE), w1a_ref[...])
        b = _dot((xg + pb_ref[...]).astype(MXU_DTYPE), w1b_ref[...])
        hid = jax.nn.gelu(a + pltpu.roll(b, nc - 1, 0))
        acc = acc + _dot(hid.astype(MXU_DTYPE), w2_ref[g])
    o_ref[...] = acc.astype(o_ref.dtype)


def _compress_v_kernel(x_ref, pa_ref, pb_ref, w1at_ref, w1bt_ref, w2t_ref, o_ref):
    nc = x_ref.shape[1]
    acc = jnp.zeros(o_ref.shape, jnp.float32)
    for g in range(NSA_GROUPS):
        xg = x_ref[g]
        a = _dot_nt(w1at_ref[...], (xg + pa_ref[...]).astype(MXU_DTYPE))
        b = _dot_nt(w1bt_ref[...], (xg + pb_ref[...]).astype(MXU_DTYPE))
        hid = jax.nn.gelu(a + pltpu.roll(b, nc - 1, 1))
        acc = acc + _dot(w2t_ref[g], hid.astype(MXU_DTYPE))
    o_ref[...] = acc.astype(o_ref.dtype)


def _compress(xc, kind, pos, w1, w2, transposed):
    b, _, nc, cw = xc.shape
    pos_flat = pos.reshape(1, CMP_BLOCK * HEAD_DIM)
    pa, pb = pos_flat[:, :cw], pos_flat[:, cw:]
    w1a, w1b = w1[:cw].astype(MXU_DTYPE), w1[cw:].astype(MXU_DTYPE)
    w2p = jnp.zeros((NSA_GROUPS, CMP_HIDDEN, LANE), jnp.float32)
    for g in range(NSA_GROUPS):
        w2p = w2p.at[g, :, g * HEAD_DIM:(g + 1) * HEAD_DIM].set(w2)
    w2p = w2p.astype(MXU_DTYPE)
    x_spec = pl.BlockSpec((None, NSA_GROUPS, nc, cw), lambda i: (i, kind, 0, 0))
    if not transposed:
        return pl.pallas_call(
            _compress_k_kernel, grid=(b,),
            in_specs=[x_spec, _const_spec(pa.shape), _const_spec(pb.shape), _const_spec(w1a.shape),
                      _const_spec(w1b.shape), _const_spec(w2p.shape)],
            out_specs=pl.BlockSpec((None, nc, LANE), lambda i: (i, 0, 0)),
            out_shape=jax.ShapeDtypeStruct((b, nc, LANE), MXU_DTYPE),
            compiler_params=_params("parallel"), name="compress_k",
        )(xc, pa, pb, w1a, w1b, w2p)
    w1at, w1bt, w2pt = w1a.T, w1b.T, jnp.swapaxes(w2p, 1, 2)
    return pl.pallas_call(
        _compress_v_kernel, grid=(b,),
        in_specs=[x_spec, _const_spec(pa.shape), _const_spec(pb.shape), _const_spec(w1at.shape),
                  _const_spec(w1bt.shape), _const_spec(w2pt.shape)],
        out_specs=pl.BlockSpec((None, LANE, nc), lambda i: (i, 0, 0)),
        out_shape=jax.ShapeDtypeStruct((b, LANE, nc), MXU_DTYPE),
        compiler_params=_params("parallel"), name="compress_v",
    )(xc, pa, pb, w1at, w1bt, w2pt)


def _group_queries_t(qt):
    z = jnp.zeros((HEAD_DIM, qt.shape[1]), qt.dtype)
    cols = []
    for h in range(NSA_HEADS):
        slab = qt[h * HEAD_DIM:(h + 1) * HEAD_DIM]
        cols.append(jnp.concatenate([slab, z] if h < HPG else [z, slab], axis=0))
    return jnp.concatenate(cols, axis=1)


def _nsa_kernel(qat_ref, qart_ref, gt_ref, kc_ref, vct_ref, kslc_ref, vslct_ref, kwin_ref, vwint_ref,
                ovt_ref, o_ref, sel_ref, m_ref, l_ref, acc_ref):
    qi = pl.program_id(1)
    tq = NSA_TQ
    nc = kc_ref.shape[0]
    t0 = qi * tq
    tok = t0 + lax.broadcasted_iota(jnp.int32, (1, tq), 1)
    qc = _group_queries_t(qat_ref[...])
    qr = _group_queries_t(qart_ref[...])

    s_c = _dot(kc_ref[...], qc)
    cmp_end = lax.broadcasted_iota(jnp.int32, (nc, 1), 0) * CMP_STRIDE + (CMP_BLOCK - 1)
    mask_c = cmp_end <= tok
    probs, psum = [], []
    for h in range(NSA_HEADS):
        s = jnp.where(mask_c, s_c[:, h * tq:(h + 1) * tq], NEG)
        m = jnp.max(s, axis=0, keepdims=True)
        p = jnp.where(mask_c, jnp.exp(s - m), 0.0)
        l = jnp.sum(p, axis=0, keepdims=True)
        p = p * (1.0 / jnp.where(l > 0.0, l, 1.0))
        probs.append(p)
        if h % HPG == 0:
            psum.append(p)
        else:
            psum[-1] = psum[-1] + p
    o_cmp = _dot(vct_ref[...], jnp.concatenate(probs, axis=1).astype(MXU_DTYPE))
    ps = jnp.concatenate(psum, axis=1)
    ps_hi = ps.astype(MXU_DTYPE)
    ps_lo = (ps - ps_hi.astype(jnp.float32)).astype(MXU_DTYPE)
    imp = _dot(ovt_ref[...], ps_hi) + _dot(ovt_ref[...], ps_lo)

    tok2 = jnp.concatenate([tok, tok], axis=1)
    blk = lax.broadcasted_iota(jnp.int32, (NSP, 1), 0)
    blk_f = blk.astype(jnp.float32)
    cur = tok2 // SLC_BLOCK
    forced = (blk == 0) | (blk == cur) | (blk == cur - 1)
    score = jnp.where(forced, BIG, jnp.where(blk * SLC_BLOCK <= tok2, imp, -BIG))
    sel = jnp.zeros(score.shape, jnp.float32)
    for _ in range(SLC_TOPK):
        best = jnp.max(score, axis=0, keepdims=True)
        first = jnp.min(jnp.where(score == best, blk_f, float(NSP)), axis=0, keepdims=True)
        pick = blk_f == first
        sel = jnp.where(pick, 1.0, sel)
        score = jnp.where(pick, REMOVED, score)
    sel_ref[...] = sel

    m_ref[...] = jnp.full(m_ref.shape, NEG, jnp.float32)
    l_ref[...] = jnp.zeros(l_ref.shape, jnp.float32)
    acc_ref[...] = jnp.zeros(acc_ref.shape, jnp.float32)
    bpt = NSA_TK // SLC_BLOCK
    cpt = NSA_TK // KCH

    def slc_tile(j, carry):
        k0 = pl.multiple_of(j * NSA_TK, NSA_TK)
        s = _dot(kslc_ref[pl.ds(k0, NSA_TK), :], qr)
        kpos = k0 + lax.broadcasted_iota(jnp.int32, (NSA_TK, 1), 0)
        causal = kpos <= tok
        selrows = sel_ref[pl.ds(pl.multiple_of(j * bpt, bpt), bpt), :]
        selexp = jnp.concatenate(
            [jnp.broadcast_to(selrows[i:i + 1], (SLC_BLOCK, 2 * tq)) for i in range(bpt)], axis=0)
        valid = [causal & (selexp[:, g * tq:(g + 1) * tq] > 0.5) for g in range(NSA_GROUPS)]
        s = jnp.concatenate(
            [jnp.where(valid[h // HPG], s[:, h * tq:(h + 1) * tq], NEG) for h in range(NSA_HEADS)], axis=1)
        m_old = m_ref[...]
        m_new = jnp.maximum(m_old, jnp.max(s, axis=0, keepdims=True))
        alpha = jnp.exp(m_old - m_new)
        p = jnp.exp(s - m_new)
        l_ref[...] = alpha * l_ref[...] + jnp.sum(p, axis=0, keepdims=True)
        m_ref[...] = m_new
        vt = jnp.concatenate([vslct_ref[j * cpt + c] for c in range(cpt)], axis=1)
        acc_ref[...] = acc_ref[...] * alpha + _dot(vt, p.astype(MXU_DTYPE))
        return carry

    lax.fori_loop(0, (t0 + tq - 1) // NSA_TK + 1, slc_tile, 0)
    o_slc = acc_ref[...] * (1.0 / l_ref[...])

    nwc = (WINDOW + tq) // KCH
    c0 = jnp.maximum(qi * (tq // KCH) - WINDOW // KCH, 0)
    w0 = pl.multiple_of(c0 * KCH, KCH)
    s = _dot(kwin_ref[pl.ds(w0, WINDOW + tq), :], qr)
    kpos = w0 + lax.broadcasted_iota(jnp.int32, (WINDOW + tq, 1), 0)
    valid_w = (kpos <= tok) & (kpos > tok - WINDOW)
    s = jnp.concatenate(
        [jnp.where(valid_w, s[:, h * tq:(h + 1) * tq], NEG) for h in range(NSA_HEADS)], axis=1)
    p = jnp.exp(s - jnp.max(s, axis=0, keepdims=True))
    l = jnp.sum(p, axis=0, keepdims=True)
    vt = jnp.concatenate([vwint_ref[c0 + c] for c in range(nwc)], axis=1)
    o_win = _dot(vt, p.astype(MXU_DTYPE)) * (1.0 / l)

    gt = gt_ref[...]
    outs = []
    for h in range(NSA_HEADS):
        rows = slice((h // HPG) * HEAD_DIM, (h // HPG + 1) * HEAD_DIM)
        cols = slice(h * tq, (h + 1) * tq)
        outs.append(gt[3 * h:3 * h + 1] * o_cmp[rows, cols] + gt[3 * h + 1:3 * h + 2] * o_slc[rows, cols]
                    + gt[3 * h + 2:3 * h + 3] * o_win[rows, cols])
    o_ref[...] = jnp.concatenate(outs, axis=0).T.astype(o_ref.dtype)


def _nsa(bsz, seq, qat, qart, gt, kc, vct, kslc, vslct, kwin, vwint, ovt):
    n = bsz * seq
    nq = seq // NSA_TQ
    nc = kc.shape[1]
    qspec = lambda r: pl.BlockSpec((r, NSA_TQ), lambda b, i: (0, b * nq + i))
    kspec = pl.BlockSpec((seq, LANE), lambda b, i: (b, 0))
    vspec = pl.BlockSpec((seq // KCH, LANE, KCH), lambda b, i: (b, 0, 0))
    return pl.pallas_call(
        _nsa_kernel,
        grid=(bsz, nq),
        in_specs=[qspec(512), qspec(512), qspec(G_ROWS),
                  pl.BlockSpec((None, nc, LANE), lambda b, i: (b, 0, 0)),
                  pl.BlockSpec((None, LANE, nc), lambda b, i: (b, 0, 0)),
                  kspec, vspec, kspec, vspec, _const_spec(ovt.shape)],
        out_specs=pl.BlockSpec((NSA_TQ, 512), lambda b, i: (b * nq + i, 0)),
        out_shape=jax.ShapeDtypeStruct((n, 512), MXU_DTYPE),
        scratch_shapes=[pltpu.VMEM((NSP, NSA_GROUPS * NSA_TQ), jnp.float32),
                        pltpu.VMEM((1, NSA_HEADS * NSA_TQ), jnp.float32),
                        pltpu.VMEM((1, NSA_HEADS * NSA_TQ), jnp.float32),
                        pltpu.VMEM((LANE, NSA_HEADS * NSA_TQ), jnp.float32)],
        compiler_params=_params("parallel", "parallel"),
        name="nsa_attention",
    )(qat, qart, gt, kc, vct, kslc, vslct, kwin, vwint, ovt)


def _diff_kernel(lambda_init, q_ref, k_ref, vt_ref, lq1_ref, lk1_ref, lq2_ref, lk2_ref, subln_ref,
                 o_ref, m_ref, l_ref, acc_ref):
    qi = pl.program_id(2)
    tq, tk = DIFF_TQ, DIFF_TK
    cpt = tk // KCH
    t0 = qi * tq
    qt = q_ref[...]
    z = jnp.zeros((HEAD_DIM, tq), qt.dtype)
    qp = jnp.concatenate([jnp.concatenate([qt[:HEAD_DIM], z], axis=0),
                          jnp.concatenate([z, qt[HEAD_DIM:]], axis=0)], axis=1)
    tok = t0 + lax.broadcasted_iota(jnp.int32, (1, tq), 1)
    tok2 = jnp.concatenate([tok, tok], axis=1)

    m_ref[...] = jnp.full(m_ref.shape, NEG, jnp.float32)
    l_ref[...] = jnp.zeros(l_ref.shape, jnp.float32)
    acc_ref[...] = jnp.zeros(acc_ref.shape, jnp.float32)

    def tile(j, masked):
        k0 = pl.multiple_of(j * tk, tk)
        s = _dot(k_ref[pl.ds(k0, tk), :], qp)
        if masked:
            kpos = k0 + lax.broadcasted_iota(jnp.int32, (tk, 1), 0)
            s = jnp.where(kpos <= tok2, s, NEG)
        m_old = m_ref[...]
        m_new = jnp.maximum(m_old, jnp.max(s, axis=0, keepdims=True))
        alpha = jnp.exp(m_old - m_new)
        p = jnp.exp(s - m_new)
        l_ref[...] = alpha * l_ref[...] + jnp.sum(p, axis=0, keepdims=True)
        m_ref[...] = m_new
        vt = jnp.concatenate([vt_ref[j * cpt + c] for c in range(cpt)], axis=1)
        acc_ref[...] = acc_ref[...] * alpha + _dot(vt, p.astype(MXU_DTYPE))

    last = t0 // tk

    def body(j, carry):
        tile(j, False)
        return carry

    lax.fori_loop(0, last, body, 0)
    tile(last, True)

    lam = (jnp.exp(jnp.sum(lq1_ref[...] * lk1_ref[...], axis=1, keepdims=True))
           - jnp.exp(jnp.sum(lq2_ref[...] * lk2_ref[...], axis=1, keepdims=True)) + lambda_init)
    o = acc_ref[...] * (1.0 / l_ref[...])
    o = o[:, :tq] - lam * o[:, tq:]
    o = o * lax.rsqrt(jnp.mean(o * o, axis=0, keepdims=True) + NORM_EPS)
    o = o * subln_ref[...] * (1.0 - lambda_init)
    o_ref[...] = o.T.astype(o_ref.dtype)


def _diff(bsz, seq, lambda_init, qbt, kb, vbt, lq1, lk1, lq2, lk2, subln):
    n = bsz * seq
    nq = seq // DIFF_TQ
    vec = _const_spec((1, HEAD_DIM))
    return pl.pallas_call(
        functools.partial(_diff_kernel, lambda_init),
        grid=(bsz, DIFF_HEADS, nq),
        in_specs=[pl.BlockSpec((DIFF_V_DIM, DIFF_TQ), lambda b, h, i: (h, b * nq + i)),
                  pl.BlockSpec((seq, LANE), lambda b, h, i: (b, h)),
                  pl.BlockSpec((seq // KCH, DIFF_V_DIM, KCH), lambda b, h, i: (b, h, 0)),
                  vec, vec, vec, vec, _const_spec((DIFF_V_DIM, 1))],
        out_specs=pl.BlockSpec((DIFF_TQ, DIFF_V_DIM), lambda b, h, i: (b * nq + i, h)),
        out_shape=jax.ShapeDtypeStruct((n, DIFF_HEADS * DIFF_V_DIM), MXU_DTYPE),
        scratch_shapes=[pltpu.VMEM((1, 2 * DIFF_TQ), jnp.float32),
                        pltpu.VMEM((1, 2 * DIFF_TQ), jnp.float32),
                        pltpu.VMEM((DIFF_V_DIM, 2 * DIFF_TQ), jnp.float32)],
        compiler_params=_params("parallel", "parallel", "parallel"),
        name="diff_attention",
    )(qbt, kb, vbt, lq1, lk1, lq2, lk2, subln)


def _merge_kernel(x_ref, g_ref, wm_ref, bm_ref, oa_ref, ob_ref, oc_ref, wa_ref, wb_ref, wc_ref, wo_ref, o_ref):
    x = x_ref[...]
    d = x.shape[1]
    hb = _rmsnorm_rows(x, g_ref[...]).astype(MXU_DTYPE)
    mixed = None
    for i, (br_ref, w_ref) in enumerate(((oa_ref, wa_ref), (ob_ref, wb_ref), (oc_ref, wc_ref))):
        cols = slice(i * d, (i + 1) * d)
        gate = jax.nn.sigmoid(_dot(hb, wm_ref[:, cols]) + bm_ref[:, cols])
        term = gate * _dot(br_ref[...], w_ref[...])
        mixed = term if mixed is None else mixed + term
    o_ref[...] = x + _dot(mixed.astype(MXU_DTYPE), wo_ref[...])


def _merge(x2, g, wm, bm, oa, ob, oc, wa, wb, wc, wo):
    n, d = x2.shape
    row = lambda w: pl.BlockSpec((TM, w), lambda i: (i, 0))
    return pl.pallas_call(
        _merge_kernel,
        grid=(n // TM,),
        in_specs=[row(d), _const_spec(g.shape), _const_spec(wm.shape), _const_spec(bm.shape),
                  row(BRANCH), row(BRANCH), row(BRANCH),
                  _const_spec(wa.shape), _const_spec(wb.shape), _const_spec(wc.shape), _const_spec(wo.shape)],
        out_specs=row(d),
        out_shape=jax.ShapeDtypeStruct((n, d), jnp.float32),
        compiler_params=_params("parallel"),
        name="merge_out",
    )(x2, g, wm, bm, oa, ob, oc, wa, wb, wc, wo)


def _ffn_kernel(final, n_chunks, x_ref, g_ref, w1_ref, w3_ref, w2_ref, gf_ref, o_ref):
    x = x_ref[...]
    hb = _rmsnorm_rows(x, g_ref[...]).astype(MXU_DTYPE)
    cw = w1_ref.shape[1] // n_chunks
    y = x
    for c in range(n_chunks):
        cols = slice(c * cw, (c + 1) * cw)
        act = jax.nn.silu(_dot(hb, w1_ref[:, cols])) * _dot(hb, w3_ref[:, cols])
        y = y + _dot(act.astype(MXU_DTYPE), w2_ref[cols, :])
    if final:
        y = _rmsnorm_rows(y, gf_ref[...])
    o_ref[...] = y


def _ffn(x2, g, w1, w3, w2, gf, final):
    n, d = x2.shape
    dff = w1.shape[1]
    n_chunks = 2 if dff % (2 * LANE) == 0 else 1
    row = pl.BlockSpec((TM, d), lambda i: (i, 0))
    return pl.pallas_call(
        functools.partial(_ffn_kernel, final, n_chunks),
        grid=(n // TM,),
        in_specs=[row, _const_spec(g.shape), _const_spec(w1.shape), _const_spec(w3.shape),
                  _const_spec(w2.shape), _const_spec(gf.shape)],
        out_specs=row,
        out_shape=jax.ShapeDtypeStruct((n, d), jnp.float32),
        compiler_params=_params("parallel"),
        name="ffn",
    )(x2, g, w1, w3, w2, gf)


def _overlap_t(seq):
    nc = seq // CMP_STRIDE
    c_start = jnp.arange(nc)[None, :] * CMP_STRIDE
    s_start = jnp.arange(NSP)[:, None] * SLC_BLOCK
    ov = (c_start < s_start + SLC_BLOCK) & (c_start + CMP_BLOCK > s_start)
    return ov.astype(MXU_DTYPE)


def kernel(x, positions, attn_norm, w_in, cmp_pos_k, cmp_k_w1, cmp_k_w2, cmp_pos_v, cmp_v_w1, cmp_v_w2, diff_lq1, diff_lk1, diff_lq2, diff_lk2, diff_subln, sgu_norm, sgu_w, sgu_b, w_branch_a, w_branch_b, w_branch_c, w_merge, b_merge, w_out, ffn_norm, w_ffn1, w_ffn3, w_ffn2, final_norm):
    bsz, seq, d = x.shape
    depth = w_in.shape[0]
    n = bsz * seq
    assert seq % NSA_TK == 0 and seq >= WINDOW + NSA_TQ and seq // SLC_BLOCK <= NSP and n % TM == 0
    bf = MXU_DTYPE
    nc = seq // CMP_STRIDE
    cw = CMP_STRIDE * HEAD_DIM

    tables = _rope_tables(positions)
    ovt = _overlap_t(seq)
    causal = jnp.tril(jnp.ones((SGU_CHUNK, SGU_CHUNK), dtype=bool))
    x2 = x.reshape(n, d)
    row = lambda v: v.reshape(1, -1)

    for l in range(depth):
        lambda_init = 0.8 - 0.6 * math.exp(-0.3 * l)
        w = w_in[l]
        wp = jnp.concatenate([w[:, 512:768], w[:, 768:896], w[:, 1024:1152], w[:, 1816:2328], w[:, 2840:3864]],
                             axis=1).astype(bf)
        wt = jnp.concatenate([w[:, 0:512], w[:, 896:1024], w[:, 1152:1280], w[:, 1280:1304],
                              jnp.zeros((d, G_ROWS - 3 * NSA_HEADS), w.dtype), w[:, 1304:1816], w[:, 2328:2840]],
                             axis=1).T.astype(bf)
        sgu_wm = jnp.where(causal[None], sgu_w[l], 0.0).astype(bf)
        sgu_bias = jnp.repeat(sgu_b[l].T, SGU_WIDTH // SGU_GROUPS, axis=1)

        (kvc, kslc, kwin, kb, oc, qat, qart, vslct, vwint, gt, qbt, vbt) = _inproj(
            x2, row(attn_norm[l]), wp, wt, tables, row(sgu_norm[l]), sgu_wm, sgu_bias)

        xc = kvc.reshape(bsz, seq, 2 * NSA_GROUPS, HEAD_DIM).transpose(0, 2, 1, 3).reshape(
            bsz, 2 * NSA_GROUPS, nc, cw)
        kc = _compress(xc, 0, cmp_pos_k[l], cmp_k_w1[l], cmp_k_w2[l], transposed=False)
        vct = _compress(xc, 1, cmp_pos_v[l], cmp_v_w1[l], cmp_v_w2[l], transposed=True)

        oa = _nsa(bsz, seq, qat, qart, gt, kc, vct, kslc, vslct, kwin, vwint, ovt)
        ob = _diff(bsz, seq, lambda_init, qbt, kb, vbt, row(diff_lq1[l]), row(diff_lk1[l]),
                   row(diff_lq2[l]), row(diff_lk2[l]), diff_subln[l].reshape(DIFF_V_DIM, 1))

        x2 = _merge(x2, row(attn_norm[l]), w_merge[l].astype(bf), row(b_merge[l]), oa, ob, oc,
                    w_branch_a[l].astype(bf), w_branch_b[l].astype(bf), w_branch_c[l].astype(bf),
                    w_out[l].astype(bf))
        x2 = _ffn(x2, row(ffn_norm[l]), w_ffn1[l].astype(bf), w_ffn3[l].astype(bf), w_ffn2[l].astype(bf),
                  row(final_norm), final=(l == depth - 1))
    return x2.reshape(bsz, seq, d)
```

```python
import functools
import math

import jax
import jax.numpy as jnp
from jax import lax
from jax.experimental import pallas as pl
from jax.experimental.pallas import tpu as pltpu

HEAD_DIM = 64
HALF = HEAD_DIM // 2
ROPE_THETA = 10000.0
NORM_EPS = 1e-6
BIG = 1e9
NEG = -1e30
REMOVED = -3e38

NSA_HEADS = 8
NSA_GROUPS = 2
HPG = NSA_HEADS // NSA_GROUPS
CMP_BLOCK = 32
CMP_STRIDE = 16
CMP_HIDDEN = 256
SLC_BLOCK = 64
SLC_TOPK = 16
WINDOW = 512
DIFF_HEADS = 4
DIFF_V_DIM = 2 * HEAD_DIM
SGU_CHUNK = 128
SGU_GROUPS = 4
SGU_WIDTH = 512
BRANCH = 512
SCALE = HEAD_DIM ** -0.5

LANE = 128
NSP = 128
MXU_DTYPE = jnp.bfloat16
VMEM_LIMIT = 56 * 1024 * 1024

TM = 512
NSA_TQ = 128
NSA_TK = 512
DIFF_TQ = 512
DIFF_TK = 512
CHAIN = 256
KCH = 128


def _params(*sem):
    return pltpu.CompilerParams(dimension_semantics=sem, vmem_limit_bytes=VMEM_LIMIT)


def _const_spec(shape):
    nd = len(shape)
    return pl.BlockSpec(shape, lambda *_: (0,) * nd, pipeline_mode=pl.Buffered(1))


def _dot(a, b):
    return jnp.dot(a, b, preferred_element_type=jnp.float32)


def _dot_nt(a, b):
    return lax.dot_general(a, b, (((1,), (1,)), ((), ())), preferred_element_type=jnp.float32)


def _rmsnorm_rows(x, g):
    return x * lax.rsqrt(jnp.mean(x * x, axis=-1, keepdims=True) + NORM_EPS) * g


def _tables_kernel(pos_col_ref, pos_row_ref, f_row_ref, f_col_ref, cos_ref, sin_ref, cost_ref, sint_ref):
    ang = pos_col_ref[...].astype(jnp.float32) * f_row_ref[...]
    lane = lax.broadcasted_iota(jnp.int32, ang.shape, 1)
    cos_ref[...] = jnp.cos(ang)
    sin_ref[...] = jnp.where((lane % HEAD_DIM) < HALF, -jnp.sin(ang), jnp.sin(ang))
    angt = f_col_ref[...] * pos_row_ref[...].astype(jnp.float32)
    cost_ref[...] = jnp.cos(angt)
    sint_ref[...] = jnp.sin(angt)


def _rope_tables(positions):
    n = positions.size
    inv_freq = ROPE_THETA ** (-jnp.arange(HALF, dtype=jnp.float32) / HALF)
    f_row = jnp.tile(inv_freq, LANE // HALF).reshape(1, LANE)
    f_col = inv_freq.reshape(HALF, 1)
    pos_col = positions.reshape(n, 1)
    pos_row = positions.reshape(1, n)
    return pl.pallas_call(
        _tables_kernel,
        grid=(n // TM,),
        in_specs=[pl.BlockSpec((TM, 1), lambda i: (i, 0)),
                  pl.BlockSpec((1, TM), lambda i: (0, i)),
                  _const_spec((1, LANE)), _const_spec((HALF, 1))],
        out_specs=[pl.BlockSpec((TM, LANE), lambda i: (i, 0)),
                   pl.BlockSpec((TM, LANE), lambda i: (i, 0)),
                   pl.BlockSpec((HALF, TM), lambda i: (0, i)),
                   pl.BlockSpec((HALF, TM), lambda i: (0, i))],
        out_shape=[jax.ShapeDtypeStruct((n, LANE), jnp.float32),
                   jax.ShapeDtypeStruct((n, LANE), jnp.float32),
                   jax.ShapeDtypeStruct((HALF, n), jnp.float32),
                   jax.ShapeDtypeStruct((HALF, n), jnp.float32)],
        compiler_params=_params("parallel"),
        name="rope_tables",
    )(pos_col, pos_row, f_row, f_col)


P_KVC, P_KSLC, P_KWIN, P_KB, P_UV, P_END = 0, 256, 384, 512, 1024, 2048
T_QA, T_VSLC, T_VWIN, T_G, T_QB, T_VB, T_END = 0, 512, 640, 768, 800, 1312, 1824
G_ROWS = T_QB - T_G


def _rope_plain(y, cos, sin):
    lane = lax.broadcasted_iota(jnp.int32, y.shape, 1)
    swapped = jnp.where((lane % HEAD_DIM) < HALF, pltpu.roll(y, LANE - HALF, 1), pltpu.roll(y, HALF, 1))
    return y * cos + swapped * sin


def _rope_t(y, cos, sin):
    y1, y2 = y[:HALF], y[HALF:]
    return jnp.concatenate([y1 * cos - y2 * sin, y2 * cos + y1 * sin], axis=0)


def _inproj_kernel(x_ref, g_ref, wp_ref, wt_ref, cos_ref, sin_ref, cost_ref, sint_ref,
                   sgu_g_ref, sgu_w_ref, sgu_b_ref,
                   kvc_ref, kslc_ref, kwin_ref, kb_ref, oc_ref,
                   qat_ref, qart_ref, vslct_ref, vwint_ref, gt_ref, qbt_ref, vbt_ref):
    x = x_ref[...]
    hb = _rmsnorm_rows(x, g_ref[...]).astype(MXU_DTYPE)
    cos, sin = cos_ref[...], sin_ref[...]
    cost, sint = cost_ref[...], sint_ref[...]
    tm = x.shape[0]

    kvc_ref[...] = _dot(hb, wp_ref[:, P_KVC:P_KSLC])
    kslc_ref[...] = _rope_plain(_dot(hb, wp_ref[:, P_KSLC:P_KWIN]), cos, sin).astype(kslc_ref.dtype)
    kwin_ref[...] = _rope_plain(_dot(hb, wp_ref[:, P_KWIN:P_KB]), cos, sin).astype(kwin_ref.dtype)
    kb = _dot(hb, wp_ref[:, P_KB:P_UV])
    for c in range((P_UV - P_KB) // LANE):
        sl = slice(c * LANE, (c + 1) * LANE)
        kb_ref[:, sl] = _rope_plain(kb[:, sl], cos, sin).astype(kb_ref.dtype)

    z = jax.nn.gelu(_dot(hb, wp_ref[:, P_UV:P_END]))
    u = z[:, :SGU_WIDTH]
    vn = _rmsnorm_rows(z[:, SGU_WIDTH:], sgu_g_ref[...]).astype(MXU_DTYPE)
    gdim = SGU_WIDTH // SGU_GROUPS
    for ci in range(tm // SGU_CHUNK):
        rows = slice(ci * SGU_CHUNK, (ci + 1) * SGU_CHUNK)
        for gi in range(SGU_GROUPS):
            cols = slice(gi * gdim, (gi + 1) * gdim)
            s = _dot(sgu_w_ref[gi], vn[rows, cols]) + sgu_b_ref[:, cols]
            oc_ref[rows, cols] = (u[rows, cols] * s).astype(oc_ref.dtype)

    qat = _dot_nt(wt_ref[T_QA:T_VSLC, :], hb) * SCALE
    qat_ref[...] = qat.astype(qat_ref.dtype)
    for h in range(NSA_HEADS):
        sl = slice(h * HEAD_DIM, (h + 1) * HEAD_DIM)
        qart_ref[sl, :] = _rope_t(qat[sl], cost, sint).astype(qart_ref.dtype)
    vst = _dot_nt(wt_ref[T_VSLC:T_VWIN, :], hb).astype(vslct_ref.dtype)
    vwt = _dot_nt(wt_ref[T_VWIN:T_G, :], hb).astype(vwint_ref.dtype)
    for c in range(tm // KCH):
        sl = slice(c * KCH, (c + 1) * KCH)
        vslct_ref[c] = vst[:, sl]
        vwint_ref[c] = vwt[:, sl]
    gt_ref[...] = jax.nn.sigmoid(_dot_nt(wt_ref[T_G:T_QB, :], hb))
    qbt = _dot_nt(wt_ref[T_QB:T_VB, :], hb) * SCALE
    for h in range(2 * DIFF_HEADS):
        sl = slice(h * HEAD_DIM, (h + 1) * HEAD_DIM)
        qbt_ref[sl, :] = _rope_t(qbt[sl], cost, sint).astype(qbt_ref.dtype)
    vbt = _dot_nt(wt_ref[T_VB:T_END, :], hb).astype(vbt_ref.dtype)
    for c in range(tm // KCH):
        vbt_ref[c] = vbt[:, c * KCH:(c + 1) * KCH]


def _inproj(x2, g, wp, wt, tables, sgu_g, sgu_w, sgu_b):
    n, d = x2.shape
    cos, sin, cost, sint = tables
    row = lambda w: pl.BlockSpec((TM, w), lambda i: (i, 0))
    colt = lambda r: pl.BlockSpec((r, TM), lambda i: (0, i))
    chunk = lambda r: pl.BlockSpec((TM // KCH, r, KCH), lambda i: (i, 0, 0))
    bf = MXU_DTYPE
    outs = [
        (row(256), jax.ShapeDtypeStruct((n, 256), jnp.float32)),
        (row(LANE), jax.ShapeDtypeStruct((n, LANE), bf)),
        (row(LANE), jax.ShapeDtypeStruct((n, LANE), bf)),
        (row(512), jax.ShapeDtypeStruct((n, 512), bf)),
        (row(SGU_WIDTH), jax.ShapeDtypeStruct((n, SGU_WIDTH), bf)),
        (colt(512), jax.ShapeDtypeStruct((512, n), bf)),
        (colt(512), jax.ShapeDtypeStruct((512, n), bf)),
        (chunk(LANE), jax.ShapeDtypeStruct((n // KCH, LANE, KCH), bf)),
        (chunk(LANE), jax.ShapeDtypeStruct((n // KCH, LANE, KCH), bf)),
        (colt(G_ROWS), jax.ShapeDtypeStruct((G_ROWS, n), jnp.float32)),
        (colt(512), jax.ShapeDtypeStruct((512, n), bf)),
        (chunk(512), jax.ShapeDtypeStruct((n // KCH, 512, KCH), bf)),
    ]
    return pl.pallas_call(
        _inproj_kernel,
        grid=(n // TM,),
        in_specs=[row(d), _const_spec((1, d)), _const_spec(wp.shape), _const_spec(wt.shape),
                  row(LANE), row(LANE), colt(HALF), colt(HALF),
                  _const_spec(sgu_g.shape), _const_spec(sgu_w.shape), _const_spec(sgu_b.shape)],
        out_specs=[o[0] for o in outs],
        out_shape=[o[1] for o in outs],
        compiler_params=_params("parallel"),
        name="inproj",
    )(x2, g, wp, wt, cos, sin, cost, sint, sgu_g, sgu_w, sgu_b)


def _compress_k_kernel(x_ref, pa_ref, pb_ref, w1a_ref, w1b_ref, w2_ref, o_ref):
    nc = x_ref.shape[1]
    acc = jnp.zeros(o_ref.shape, jnp.float32)
    for g in range(NSA_GROUPS):
        xg = x_ref[g]
        a = _dot((xg + pa_ref[...]).astype(MXU_DTYPE), w1a_ref[...])
        b = _dot((xg + pb_ref[...]).astype(MXU_DTYPE), w1b_ref[...])
        hid = jax.nn.gelu(a + pltpu.roll(b, nc - 1, 0))
        acc = acc + _dot(hid.astype(MXU_DTYPE), w2_ref[g])
    o_ref[...] = acc.astype(o_ref.dtype)


def _compress_v_kernel(x_ref, pa_ref, pb_ref, w1at_ref, w1bt_ref, w2t_ref, o_ref):
    nc = x_ref.shape[1]
    acc = jnp.zeros(o_ref.shape, jnp.float32)
    for g in range(NSA_GROUPS):
        xg = x_ref[g]
        a = _dot_nt(w1at_ref[...], (xg + pa_ref[...]).astype(MXU_DTYPE))
        b = _dot_nt(w1bt_ref[...], (xg + pb_ref[...]).astype(MXU_DTYPE))
        hid = jax.nn.gelu(a + pltpu.roll(b, nc - 1, 1))
        acc = acc + _dot(w2t_ref[g], hid.astype(MXU_DTYPE))
    o_ref[...] = acc.astype(o_ref.dtype)


def _compress(xc, kind, pos, w1, w2, transposed):
    b, _, nc, cw = xc.shape
    pos_flat = pos.reshape(1, CMP_BLOCK * HEAD_DIM)
    pa, pb = pos_flat[:, :cw], pos_flat[:, cw:]
    w1a, w1b = w1[:cw].astype(MXU_DTYPE), w1[cw:].astype(MXU_DTYPE)
    w2p = jnp.zeros((NSA_GROUPS, CMP_HIDDEN, LANE), jnp.float32)
    for g in range(NSA_GROUPS):
        w2p = w2p.at[g, :, g * HEAD_DIM:(g + 1) * HEAD_DIM].set(w2)
    w2p = w2p.astype(MXU_DTYPE)
    x_spec = pl.BlockSpec((None, NSA_GROUPS, nc, cw), lambda i: (i, kind, 0, 0))
    if not transposed:
        return pl.pallas_call(
            _compress_k_kernel, grid=(b,),
            in_specs=[x_spec, _const_spec(pa.shape), _const_spec(pb.shape), _const_spec(w1a.shape),
                      _const_spec(w1b.shape), _const_spec(w2p.shape)],
            out_specs=pl.BlockSpec((None, nc, LANE), lambda i: (i, 0, 0)),
            out_shape=jax.ShapeDtypeStruct((b, nc, LANE), MXU_DTYPE),
            compiler_params=_params("parallel"), name="compress_k",
        )(xc, pa, pb, w1a, w1b, w2p)
    w1at, w1bt, w2pt = w1a.T, w1b.T, jnp.swapaxes(w2p, 1, 2)
    return pl.pallas_call(
        _compress_v_kernel, grid=(b,),
        in_specs=[x_spec, _const_spec(pa.shape), _const_spec(pb.shape), _const_spec(w1at.shape),
                  _const_spec(w1bt.shape), _const_spec(w2pt.shape)],
        out_specs=pl.BlockSpec((None, LANE, nc), lambda i: (i, 0, 0)),
        out_shape=jax.ShapeDtypeStruct((b, LANE, nc), MXU_DTYPE),
        compiler_params=_params("parallel"), name="compress_v",
    )(xc, pa, pb, w1at, w1bt, w2pt)


def _group_queries_t(qt):
    z = jnp.zeros((HEAD_DIM, qt.shape[1]), qt.dtype)
    cols = []
    for h in range(NSA_HEADS):
        slab = qt[h * HEAD_DIM:(h + 1) * HEAD_DIM]
        cols.append(jnp.concatenate([slab, z] if h < HPG else [z, slab], axis=0))
    return jnp.concatenate(cols, axis=1)


def _pipelined_chains(n_chain, qk, softmax, pv):
    outs = []
    s_next = qk(0)
    pending = None
    for c in range(n_chain):
        s = s_next
        if c + 1 < n_chain:
            s_next = qk(c + 1)
        r = softmax(c, s)
        if pending is not None:
            outs.append(pv(*pending))
        pending = (c, r)
    outs.append(pv(*pending))
    return outs


def _nsa_kernel(qat_ref, qart_ref, gt_ref, kc_ref, vct_ref, kslc_ref, vslct_ref, kwin_ref, vwint_ref,
                ovt_ref, e_ref, o_ref, m_ref, l_ref, acc_ref, sa_ref, sb_ref):
    qi = pl.program_id(1)
    tq, tk = NSA_TQ, NSA_TK
    nc = kc_ref.shape[0]
    n_chain = NSA_HEADS * tq // CHAIN
    hpc = CHAIN // tq
    t0 = qi * tq
    tok = t0 + lax.broadcasted_iota(jnp.int32, (1, tq), 1)
    tok_c = jnp.concatenate([tok] * hpc, axis=1)
    qc = _group_queries_t(qat_ref[...])
    qr = _group_queries_t(qart_ref[...])
    chain_cols = [slice(c * CHAIN, (c + 1) * CHAIN) for c in range(n_chain)]
    chain_group = [(c * hpc) // HPG for c in range(n_chain)]

    cmp_end = lax.broadcasted_iota(jnp.int32, (nc, 1), 0) * CMP_STRIDE + (CMP_BLOCK - 1)
    mask_c = cmp_end <= tok_c
    psum = [None] * NSA_GROUPS

    def cmp_softmax(c, s):
        s = jnp.where(mask_c, s, NEG)
        p = jnp.where(mask_c, jnp.exp(s - jnp.max(s, axis=0, keepdims=True)), 0.0)
        l = jnp.sum(p, axis=0, keepdims=True)
        p = p * (1.0 / jnp.where(l > 0.0, l, 1.0))
        ph = p[:, :tq]
        for i in range(1, hpc):
            ph = ph + p[:, i * tq:(i + 1) * tq]
        g = chain_group[c]
        psum[g] = ph if psum[g] is None else psum[g] + ph
        return p.astype(MXU_DTYPE)

    o_cmp = jnp.concatenate(_pipelined_chains(
        n_chain, lambda c: _dot(kc_ref[...], qc[:, chain_cols[c]]), cmp_softmax,
        lambda c, p: _dot(vct_ref[...], p)), axis=1)
    ps = jnp.concatenate(psum, axis=1)
    ps_hi = ps.astype(MXU_DTYPE)
    ps_lo = (ps - ps_hi.astype(jnp.float32)).astype(MXU_DTYPE)
    imp = _dot(ovt_ref[...], ps_hi) + _dot(ovt_ref[...], ps_lo)

    tok2 = jnp.concatenate([tok] * NSA_GROUPS, axis=1)
    blk = lax.broadcasted_iota(jnp.int32, (NSP, 1), 0)
    blk_f = blk.astype(jnp.float32)
    cur = tok2 // SLC_BLOCK
    forced = (blk == 0) | (blk == cur) | (blk == cur - 1)
    score = jnp.where(forced, BIG, jnp.where(blk * SLC_BLOCK <= tok2, imp, -BIG))
    bias = jnp.full(score.shape, NEG, jnp.float32)
    for _ in range(SLC_TOPK):
        best = jnp.max(score, axis=0, keepdims=True)
        first = jnp.min(jnp.where(score == best, blk_f, float(NSP)), axis=0, keepdims=True)
        pick = blk_f == first
        bias = jnp.where(pick, 0.0, bias)
        score = jnp.where(pick, REMOVED, score)
    bias = bias.astype(MXU_DTYPE)

    qa = []
    for c in range(n_chain):
        g = chain_group[c]
        bias_c = jnp.concatenate([bias[:, g * tq:(g + 1) * tq]] * hpc, axis=1)
        qa.append(jnp.concatenate([qr[:, chain_cols[c]], bias_c], axis=0))
    m_ref[...] = jnp.full(m_ref.shape, NEG, jnp.float32)
    l_ref[...] = jnp.zeros(l_ref.shape, jnp.float32)
    acc_ref[...] = jnp.zeros(acc_ref.shape, jnp.float32)
    cpt = tk // KCH

    def slc_qk(j, c):
        k0 = pl.multiple_of(j * tk, tk)
        ka = jnp.concatenate([kslc_ref[pl.ds(k0, tk), :], e_ref[pl.ds(k0, tk), :]], axis=1)
        return _dot(ka, qa[c])

    def slc_tile(j, cur_ref, nxt_ref, masked):
        vt = jnp.concatenate([vslct_ref[j * cpt + c] for c in range(cpt)], axis=1)
        kpos = j * tk + lax.broadcasted_iota(jnp.int32, (tk, 1), 0)
        m_all, l_all, acc_all = m_ref[...], l_ref[...], acc_ref[...]
        m_out, l_out, acc_out = [], [], []
        pending = None
        for c in range(n_chain):
            cols = chain_cols[c]
            if nxt_ref is not None:
                nxt_ref[c] = slc_qk(jnp.maximum(j - 1, 0), c)
            s = cur_ref[c]
            if masked:
                s = jnp.where(kpos <= tok_c, s, NEG)
            m_old = m_all[:, cols]
            m_new = jnp.maximum(m_old, jnp.max(s, axis=0, keepdims=True))
            alpha = jnp.exp(m_old - m_new)
            p = jnp.exp(s - m_new)
            l_out.append(alpha * l_all[:, cols] + jnp.sum(p, axis=0, keepdims=True))
            m_out.append(m_new)
            if pending is not None:
                acc_out.append(pending[0] + _dot(vt, pending[1]))
            pending = (acc_all[:, cols] * alpha, p.astype(MXU_DTYPE))
        acc_out.append(pending[0] + _dot(vt, pending[1]))
        m_ref[...] = jnp.concatenate(m_out, axis=1)
        l_ref[...] = jnp.concatenate(l_out, axis=1)
        acc_ref[...] = jnp.concatenate(acc_out, axis=1)

    last = t0 // tk
    for c in range(n_chain):
        sa_ref[c] = slc_qk(last, c)
    slc_tile(last, sa_ref, sb_ref, True)

    def pair(i, carry):
        j = last - 1 - 2 * i
        slc_tile(j, sb_ref, sa_ref, False)
        slc_tile(j - 1, sa_ref, sb_ref, False)
        return carry

    lax.fori_loop(0, last // 2, pair, 0)

    @pl.when(last % 2 == 1)
    def _():
        slc_tile(0, sb_ref, None, False)

    o_slc = acc_ref[...] * (1.0 / l_ref[...])

    nwc = (WINDOW + tq) // KCH
    c0 = jnp.maximum(qi * (tq // KCH) - WINDOW // KCH, 0)
    w0 = pl.multiple_of(c0 * KCH, KCH)
    kpos = w0 + lax.broadcasted_iota(jnp.int32, (WINDOW + tq, 1), 0)
    valid_w = (kpos <= tok_c) & (kpos > tok_c - WINDOW)
    vt_w = jnp.concatenate([vwint_ref[c0 + c] for c in range(nwc)], axis=1)

    def win_softmax(c, s):
        s = jnp.where(valid_w, s, NEG)
        p = jnp.exp(s - jnp.max(s, axis=0, keepdims=True))
        return p.astype(MXU_DTYPE), 1.0 / jnp.sum(p, axis=0, keepdims=True)

    o_win = jnp.concatenate(_pipelined_chains(
        n_chain, lambda c: _dot(kwin_ref[pl.ds(w0, WINDOW + tq), :], qr[:, chain_cols[c]]), win_softmax,
        lambda c, r: _dot(vt_w, r[0]) * r[1]), axis=1)

    gt = gt_ref[...]
    outs = []
    for h in range(NSA_HEADS):
        rows = slice((h // HPG) * HEAD_DIM, (h // HPG + 1) * HEAD_DIM)
        cols = slice(h * tq, (h + 1) * tq)
        outs.append(gt[3 * h:3 * h + 1] * o_cmp[rows, cols] + gt[3 * h + 1:3 * h + 2] * o_slc[rows, cols]
                    + gt[3 * h + 2:3 * h + 3] * o_win[rows, cols])
    o_ref[...] = jnp.concatenate(outs, axis=0).T.astype(o_ref.dtype)


def _nsa(bsz, seq, qat, qart, gt, kc, vct, kslc, vslct, kwin, vwint, ovt, onehot):
    n = bsz * seq
    nq = seq // NSA_TQ
    nc = kc.shape[1]
    n_chain = NSA_HEADS * NSA_TQ // CHAIN
    qspec = lambda r: pl.BlockSpec((r, NSA_TQ), lambda b, i: (0, b * nq + i))
    kspec = pl.BlockSpec((seq, LANE), lambda b, i: (b, 0))
    vspec = pl.BlockSpec((seq // KCH, LANE, KCH), lambda b, i: (b, 0, 0))
    return pl.pallas_call(
        _nsa_kernel,
        grid=(bsz, nq),
        in_specs=[qspec(512), qspec(512), qspec(G_ROWS),
                  pl.BlockSpec((None, nc, LANE), lambda b, i: (b, 0, 0)),
                  pl.BlockSpec((None, LANE, nc), lambda b, i: (b, 0, 0)),
                  kspec, vspec, kspec, vspec, _const_spec(ovt.shape), _const_spec(onehot.shape)],
        out_specs=pl.BlockSpec((NSA_TQ, 512), lambda b, i: (b * nq + i, 0)),
        out_shape=jax.ShapeDtypeStruct((n, 512), MXU_DTYPE),
        scratch_shapes=[pltpu.VMEM((1, NSA_HEADS * NSA_TQ), jnp.float32),
                        pltpu.VMEM((1, NSA_HEADS * NSA_TQ), jnp.float32),
                        pltpu.VMEM((LANE, NSA_HEADS * NSA_TQ), jnp.float32),
                        pltpu.VMEM((n_chain, NSA_TK, CHAIN), jnp.float32),
                        pltpu.VMEM((n_chain, NSA_TK, CHAIN), jnp.float32)],
        compiler_params=_params("parallel", "parallel"),
        name="nsa_attention",
    )(qat, qart, gt, kc, vct, kslc, vslct, kwin, vwint, ovt, onehot)


def _diff_kernel(lambda_init, q_ref, k_ref, vt_ref, lq1_ref, lk1_ref, lq2_ref, lk2_ref, subln_ref,
                 o_ref, m_ref, l_ref, acc_ref, sa_ref, sb_ref):
    qi = pl.program_id(2)
    tq, tk = DIFF_TQ, DIFF_TK
    cpt = tk // KCH
    t0 = qi * tq
    qt = q_ref[...]
    z = jnp.zeros((HEAD_DIM, tq), qt.dtype)
    qp = jnp.concatenate([jnp.concatenate([qt[:HEAD_DIM], z], axis=0),
                          jnp.concatenate([z, qt[HEAD_DIM:]], axis=0)], axis=1)
    tok = t0 + lax.broadcasted_iota(jnp.int32, (1, CHAIN), 1)

    m_ref[...] = jnp.full(m_ref.shape, NEG, jnp.float32)
    l_ref[...] = jnp.zeros(l_ref.shape, jnp.float32)
    acc_ref[...] = jnp.zeros(acc_ref.shape, jnp.float32)

    n_chain = 2 * tq // CHAIN

    def qk(j, c):
        k = k_ref[pl.ds(pl.multiple_of(j * tk, tk), tk), :]
        return _dot(k, qp[:, c * CHAIN:(c + 1) * CHAIN])

    def tile(j, cur_ref, nxt_ref, masked):
        vt = jnp.concatenate([vt_ref[j * cpt + c] for c in range(cpt)], axis=1)
        kpos = j * tk + lax.broadcasted_iota(jnp.int32, (tk, 1), 0)
        m_all, l_all, acc_all = m_ref[...], l_ref[...], acc_ref[...]
        m_out, l_out, acc_out = [], [], []
        pending = None
        for c in range(n_chain):
            cols = slice(c * CHAIN, (c + 1) * CHAIN)
            if nxt_ref is not None:
                nxt_ref[c] = qk(jnp.maximum(j - 1, 0), c)
            s = cur_ref[c]
            if masked:
                s = jnp.where(kpos <= tok + (c * CHAIN) % tq, s, NEG)
            m_old = m_all[:, cols]
            m_new = jnp.maximum(m_old, jnp.max(s, axis=0, keepdims=True))
            alpha = jnp.exp(m_old - m_new)
            p = jnp.exp(s - m_new)
            l_out.append(alpha * l_all[:, cols] + jnp.sum(p, axis=0, keepdims=True))
            m_out.append(m_new)
            if pending is not None:
                acc_out.append(pending[0] + _dot(vt, pending[1]))
            pending = (acc_all[:, cols] * alpha, p.astype(MXU_DTYPE))
        acc_out.append(pending[0] + _dot(vt, pending[1]))
        m_ref[...] = jnp.concatenate(m_out, axis=1)
        l_ref[...] = jnp.concatenate(l_out, axis=1)
        acc_ref[...] = jnp.concatenate(acc_out, axis=1)

    last = t0 // tk
    for c in range(n_chain):
        sa_ref[c] = qk(last, c)
    tile(last, sa_ref, sb_ref, True)

    def pair(i, carry):
        j = last - 1 - 2 * i
        tile(j, sb_ref, sa_ref, False)
        tile(j - 1, sa_ref, sb_ref, False)
        return carry

    lax.fori_loop(0, last // 2, pair, 0)

    @pl.when(last % 2 == 1)
    def _():
        tile(0, sb_ref, None, False)

    lam = (jnp.exp(jnp.sum(lq1_ref[...] * lk1_ref[...], axis=1, keepdims=True))
           - jnp.exp(jnp.sum(lq2_ref[...] * lk2_ref[...], axis=1, keepdims=True)) + lambda_init)
    o = acc_ref[...] * (1.0 / l_ref[...])
    o = o[:, :tq] - lam * o[:, tq:]
    o = o * lax.rsqrt(jnp.mean(o * o, axis=0, keepdims=True) + NORM_EPS)
    o = o * subln_ref[...] * (1.0 - lambda_init)
    o_ref[...] = o.T.astype(o_ref.dtype)


def _diff(bsz, seq, lambda_init, qbt, kb, vbt, lq1, lk1, lq2, lk2, subln):
    n = bsz * seq
    nq = seq // DIFF_TQ
    vec = _const_spec((1, HEAD_DIM))
    return pl.pallas_call(
        functools.partial(_diff_kernel, lambda_init),
        grid=(bsz, DIFF_HEADS, nq),
        in_specs=[pl.BlockSpec((DIFF_V_DIM, DIFF_TQ), lambda b, h, i: (h, b * nq + i)),
                  pl.BlockSpec((seq, LANE), lambda b, h, i: (b, h)),
                  pl.BlockSpec((seq // KCH, DIFF_V_DIM, KCH), lambda b, h, i: (b, h, 0)),
                  vec, vec, vec, vec, _const_spec((DIFF_V_DIM, 1))],
        out_specs=pl.BlockSpec((DIFF_TQ, DIFF_V_DIM), lambda b, h, i: (b * nq + i, h)),
        out_shape=jax.ShapeDtypeStruct((n, DIFF_HEADS * DIFF_V_DIM), MXU_DTYPE),
        scratch_shapes=[pltpu.VMEM((1, 2 * DIFF_TQ), jnp.float32),
                        pltpu.VMEM((1, 2 * DIFF_TQ), jnp.float32),
                        pltpu.VMEM((DIFF_V_DIM, 2 * DIFF_TQ), jnp.float32),
                        pltpu.VMEM((2 * DIFF_TQ // CHAIN, DIFF_TK, CHAIN), jnp.float32),
                        pltpu.VMEM((2 * DIFF_TQ // CHAIN, DIFF_TK, CHAIN), jnp.float32)],
        compiler_params=_params("parallel", "parallel", "parallel"),
        name="diff_attention",
    )(qbt, kb, vbt, lq1, lk1, lq2, lk2, subln)


def _merge_kernel(x_ref, g_ref, wm_ref, bm_ref, oa_ref, ob_ref, oc_ref, wa_ref, wb_ref, wc_ref, wo_ref, o_ref):
    x = x_ref[...]
    d = x.shape[1]
    hb = _rmsnorm_rows(x, g_ref[...]).astype(MXU_DTYPE)
    mixed = None
    for i, (br_ref, w_ref) in enumerate(((oa_ref, wa_ref), (ob_ref, wb_ref), (oc_ref, wc_ref))):
        cols = slice(i * d, (i + 1) * d)
        gate = jax.nn.sigmoid(_dot(hb, wm_ref[:, cols]) + bm_ref[:, cols])
        term = gate * _dot(br_ref[...], w_ref[...])
        mixed = term if mixed is None else mixed + term
    o_ref[...] = x + _dot(mixed.astype(MXU_DTYPE), wo_ref[...])


def _merge(x2, g, wm, bm, oa, ob, oc, wa, wb, wc, wo):
    n, d = x2.shape
    row = lambda w: pl.BlockSpec((TM, w), lambda i: (i, 0))
    return pl.pallas_call(
        _merge_kernel,
        grid=(n // TM,),
        in_specs=[row(d), _const_spec(g.shape), _const_spec(wm.shape), _const_spec(bm.shape),
                  row(BRANCH), row(BRANCH), row(BRANCH),
                  _const_spec(wa.shape), _const_spec(wb.shape), _const_spec(wc.shape), _const_spec(wo.shape)],
        out_specs=row(d),
        out_shape=jax.ShapeDtypeStruct((n, d), jnp.float32),
        compiler_params=_params("parallel"),
        name="merge_out",
    )(x2, g, wm, bm, oa, ob, oc, wa, wb, wc, wo)


def _ffn_kernel(final, n_chunks, x_ref, g_ref, w1_ref, w3_ref, w2_ref, gf_ref, o_ref):
    x = x_ref[...]
    hb = _rmsnorm_rows(x, g_ref[...]).astype(MXU_DTYPE)
    cw = w1_ref.shape[1] // n_chunks
    y = x
    for c in range(n_chunks):
        cols = slice(c * cw, (c + 1) * cw)
        act = jax.nn.silu(_dot(hb, w1_ref[:, cols])) * _dot(hb, w3_ref[:, cols])
        y = y + _dot(act.astype(MXU_DTYPE), w2_ref[cols, :])
    if final:
        y = _rmsnorm_rows(y, gf_ref[...])
    o_ref[...] = y


def _ffn(x2, g, w1, w3, w2, gf, final):
    n, d = x2.shape
    dff = w1.shape[1]
    n_chunks = 2 if dff % (2 * LANE) == 0 else 1
    row = pl.BlockSpec((TM, d), lambda i: (i, 0))
    return pl.pallas_call(
        functools.partial(_ffn_kernel, final, n_chunks),
        grid=(n // TM,),
        in_specs=[row, _const_spec(g.shape), _const_spec(w1.shape), _const_spec(w3.shape),
                  _const_spec(w2.shape), _const_spec(gf.shape)],
        out_specs=row,
        out_shape=jax.ShapeDtypeStruct((n, d), jnp.float32),
        compiler_params=_params("parallel"),
        name="ffn",
    )(x2, g, w1, w3, w2, gf)


def _overlap_t(seq):
    nc = seq // CMP_STRIDE
    c_start = jnp.arange(nc)[None, :] * CMP_STRIDE
    s_start = jnp.arange(NSP)[:, None] * SLC_BLOCK
    ov = (c_start < s_start + SLC_BLOCK) & (c_start + CMP_BLOCK > s_start)
    return ov.astype(MXU_DTYPE)


def _block_onehot(seq):
    return (jnp.arange(seq)[:, None] // SLC_BLOCK == jnp.arange(NSP)[None, :]).astype(MXU_DTYPE)


def kernel(x, positions, attn_norm, w_in, cmp_pos_k, cmp_k_w1, cmp_k_w2, cmp_pos_v, cmp_v_w1, cmp_v_w2, diff_lq1, diff_lk1, diff_lq2, diff_lk2, diff_subln, sgu_norm, sgu_w, sgu_b, w_branch_a, w_branch_b, w_branch_c, w_merge, b_merge, w_out, ffn_norm, w_ffn1, w_ffn3, w_ffn2, final_norm):
    bsz, seq, d = x.shape
    depth = w_in.shape[0]
    n = bsz * seq
    assert seq % NSA_TK == 0 and seq >= WINDOW + NSA_TQ and seq // SLC_BLOCK <= NSP and n % TM == 0
    bf = MXU_DTYPE
    nc = seq // CMP_STRIDE
    cw = CMP_STRIDE * HEAD_DIM

    tables = _rope_tables(positions)
    ovt = _overlap_t(seq)
    onehot = _block_onehot(seq)
    causal = jnp.tril(jnp.ones((SGU_CHUNK, SGU_CHUNK), dtype=bool))
    x2 = x.reshape(n, d)
    row = lambda v: v.reshape(1, -1)

    for l in range(depth):
        lambda_init = 0.8 - 0.6 * math.exp(-0.3 * l)
        w = w_in[l]
        wp = jnp.concatenate([w[:, 512:768], w[:, 768:896], w[:, 1024:1152], w[:, 1816:2328], w[:, 2840:3864]],
                             axis=1).astype(bf)
        wt = jnp.concatenate([w[:, 0:512], w[:, 896:1024], w[:, 1152:1280], w[:, 1280:1304],
                              jnp.zeros((d, G_ROWS - 3 * NSA_HEADS), w.dtype), w[:, 1304:1816], w[:, 2328:2840]],
                             axis=1).T.astype(bf)
        sgu_wm = jnp.where(causal[None], sgu_w[l], 0.0).astype(bf)
        sgu_bias = jnp.repeat(sgu_b[l].T, SGU_WIDTH // SGU_GROUPS, axis=1)

        (kvc, kslc, kwin, kb, oc, qat, qart, vslct, vwint, gt, qbt, vbt) = _inproj(
            x2, row(attn_norm[l]), wp, wt, tables, row(sgu_norm[l]), sgu_wm, sgu_bias)

        xc = kvc.reshape(bsz, seq, 2 * NSA_GROUPS, HEAD_DIM).transpose(0, 2, 1, 3).reshape(
            bsz, 2 * NSA_GROUPS, nc, cw)
        kc = _compress(xc, 0, cmp_pos_k[l], cmp_k_w1[l], cmp_k_w2[l], transposed=False)
        vct = _compress(xc, 1, cmp_pos_v[l], cmp_v_w1[l], cmp_v_w2[l], transposed=True)

        oa = _nsa(bsz, seq, qat, qart, gt, kc, vct, kslc, vslct, kwin, vwint, ovt, onehot)
        ob = _diff(bsz, seq, lambda_init, qbt, kb, vbt, row(diff_lq1[l]), row(diff_lk1[l]),
                   row(diff_lq2[l]), row(diff_lk2[l]), diff_subln[l].reshape(DIFF_V_DIM, 1))

        x2 = _merge(x2, row(attn_norm[l]), w_merge[l].astype(bf), row(b_merge[l]), oa, ob, oc,
                    w_branch_a[l].astype(bf), w_branch_b[l].astype(bf), w_branch_c[l].astype(bf),
                    w_out[l].astype(bf))
        x2 = _ffn(x2, row(ffn_norm[l]), w_ffn1[l].astype(bf), w_ffn3[l].astype(bf), w_ffn2[l].astype(bf),
                  row(final_norm), final=(l == depth - 1))
    return x2.reshape(bsz, seq, d)
```

```python
import functools
import math

import jax
import jax.numpy as jnp
from jax import lax
from jax.experimental import pallas as pl
from jax.experimental.pallas import tpu as pltpu

HEAD_DIM = 64
HALF = HEAD_DIM // 2
ROPE_THETA = 10000.0
NORM_EPS = 1e-6
BIG = 1e9
NEG = -1e30
REMOVED = -3e38

NSA_HEADS = 8
NSA_GROUPS = 2
HPG = NSA_HEADS // NSA_GROUPS
CMP_BLOCK = 32
CMP_STRIDE = 16
CMP_HIDDEN = 256
SLC_BLOCK = 64
SLC_TOPK = 16
WINDOW = 512
DIFF_HEADS = 4
DIFF_V_DIM = 2 * HEAD_DIM
SGU_CHUNK = 128
SGU_GROUPS = 4
SGU_WIDTH = 512
BRANCH = 512
SCALE = HEAD_DIM ** -0.5
QSCALE = SCALE * math.log2(math.e)
ONES_ROWS = 16

LANE = 128
NSP = 128
MXU_DTYPE = jnp.bfloat16
VMEM_LIMIT = 56 * 1024 * 1024

TM = 512
NSA_TQ = 256
NSA_TK = 512
DIFF_TQ = 512
DIFF_TK = 512
CHAIN = 256
KCH = 128


def _params(*sem):
    return pltpu.CompilerParams(dimension_semantics=sem, vmem_limit_bytes=VMEM_LIMIT)


def _const_spec(shape):
    nd = len(shape)
    return pl.BlockSpec(shape, lambda *_: (0,) * nd, pipeline_mode=pl.Buffered(1))


def _dot(a, b):
    return jnp.dot(a, b, preferred_element_type=jnp.float32)


def _dot_nt(a, b):
    return lax.dot_general(a, b, (((1,), (1,)), ((), ())), preferred_element_type=jnp.float32)


def _rmsnorm_rows(x, g):
    return x * lax.rsqrt(jnp.mean(x * x, axis=-1, keepdims=True) + NORM_EPS) * g


def _tables_kernel(pos_col_ref, pos_row_ref, f_row_ref, f_col_ref, cos_ref, sin_ref, cost_ref, sint_ref):
    ang = pos_col_ref[...].astype(jnp.float32) * f_row_ref[...]
    lane = lax.broadcasted_iota(jnp.int32, ang.shape, 1)
    cos_ref[...] = jnp.cos(ang)
    sin_ref[...] = jnp.where((lane % HEAD_DIM) < HALF, -jnp.sin(ang), jnp.sin(ang))
    angt = f_col_ref[...] * pos_row_ref[...].astype(jnp.float32)
    cost_ref[...] = jnp.cos(angt)
    sint_ref[...] = jnp.sin(angt)


def _rope_tables(positions):
    n = positions.size
    inv_freq = ROPE_THETA ** (-jnp.arange(HALF, dtype=jnp.float32) / HALF)
    f_row = jnp.tile(inv_freq, LANE // HALF).reshape(1, LANE)
    f_col = inv_freq.reshape(HALF, 1)
    pos_col = positions.reshape(n, 1)
    pos_row = positions.reshape(1, n)
    return pl.pallas_call(
        _tables_kernel,
        grid=(n // TM,),
        in_specs=[pl.BlockSpec((TM, 1), lambda i: (i, 0)),
                  pl.BlockSpec((1, TM), lambda i: (0, i)),
                  _const_spec((1, LANE)), _const_spec((HALF, 1))],
        out_specs=[pl.BlockSpec((TM, LANE), lambda i: (i, 0)),
                   pl.BlockSpec((TM, LANE), lambda i: (i, 0)),
                   pl.BlockSpec((HALF, TM), lambda i: (0, i)),
                   pl.BlockSpec((HALF, TM), lambda i: (0, i))],
        out_shape=[jax.ShapeDtypeStruct((n, LANE), jnp.float32),
                   jax.ShapeDtypeStruct((n, LANE), jnp.float32),
                   jax.ShapeDtypeStruct((HALF, n), jnp.float32),
                   jax.ShapeDtypeStruct((HALF, n), jnp.float32)],
        compiler_params=_params("parallel"),
        name="rope_tables",
    )(pos_col, pos_row, f_row, f_col)


P_KVC, P_KSLC, P_KWIN, P_KB, P_UV, P_END = 0, 256, 384, 512, 1024, 2048
T_QA, T_VSLC, T_VWIN, T_G, T_QB, T_VB, T_END = 0, 512, 640, 768, 800, 1312, 1824
G_ROWS = T_QB - T_G


def _rope_plain(y, cos, sin):
    lane = lax.broadcasted_iota(jnp.int32, y.shape, 1)
    swapped = jnp.where((lane % HEAD_DIM) < HALF, pltpu.roll(y, LANE - HALF, 1), pltpu.roll(y, HALF, 1))
    return y * cos + swapped * sin


def _rope_t(y, cos, sin):
    y1, y2 = y[:HALF], y[HALF:]
    return jnp.concatenate([y1 * cos - y2 * sin, y2 * cos + y1 * sin], axis=0)


def _inproj_kernel(x_ref, g_ref, wp_ref, wt_ref, cos_ref, sin_ref, cost_ref, sint_ref,
                   sgu_g_ref, sgu_w_ref, sgu_b_ref,
                   kvc_ref, kslc_ref, kwin_ref, kb_ref, oc_ref,
                   qat_ref, qart_ref, vslct_ref, vwint_ref, gt_ref, qbt_ref, vbt_ref):
    x = x_ref[...]
    hb = _rmsnorm_rows(x, g_ref[...]).astype(MXU_DTYPE)
    cos, sin = cos_ref[...], sin_ref[...]
    cost, sint = cost_ref[...], sint_ref[...]
    tm = x.shape[0]

    z = jax.nn.gelu(_dot(hb, wp_ref[:, P_UV:P_END]))
    u = z[:, :SGU_WIDTH]
    vn = _rmsnorm_rows(z[:, SGU_WIDTH:], sgu_g_ref[...]).astype(MXU_DTYPE)

    kvc_ref[...] = _dot(hb, wp_ref[:, P_KVC:P_KSLC])
    kslc_ref[...] = _rope_plain(_dot(hb, wp_ref[:, P_KSLC:P_KWIN]), cos, sin).astype(kslc_ref.dtype)
    kwin_ref[...] = _rope_plain(_dot(hb, wp_ref[:, P_KWIN:P_KB]), cos, sin).astype(kwin_ref.dtype)
    kb = _dot(hb, wp_ref[:, P_KB:P_UV])
    for c in range((P_UV - P_KB) // LANE):
        sl = slice(c * LANE, (c + 1) * LANE)
        kb_ref[:, sl] = _rope_plain(kb[:, sl], cos, sin).astype(kb_ref.dtype)

    qat = _dot_nt(wt_ref[T_QA:T_VSLC, :], hb) * QSCALE
    qat_ref[...] = qat.astype(qat_ref.dtype)
    for h in range(NSA_HEADS):
        sl = slice(h * HEAD_DIM, (h + 1) * HEAD_DIM)
        qart_ref[sl, :] = _rope_t(qat[sl], cost, sint).astype(qart_ref.dtype)
    vst = _dot_nt(wt_ref[T_VSLC:T_VWIN, :], hb).astype(vslct_ref.dtype)
    vwt = _dot_nt(wt_ref[T_VWIN:T_G, :], hb).astype(vwint_ref.dtype)
    for c in range(tm // KCH):
        sl = slice(c * KCH, (c + 1) * KCH)
        vslct_ref[c] = vst[:, sl]
        vwint_ref[c] = vwt[:, sl]
    gt_ref[...] = jax.nn.sigmoid(_dot_nt(wt_ref[T_G:T_QB, :], hb))
    qbt = _dot_nt(wt_ref[T_QB:T_VB, :], hb) * QSCALE
    for h in range(2 * DIFF_HEADS):
        sl = slice(h * HEAD_DIM, (h + 1) * HEAD_DIM)
        qbt_ref[sl, :] = _rope_t(qbt[sl], cost, sint).astype(qbt_ref.dtype)
    vbt = _dot_nt(wt_ref[T_VB:T_END, :], hb).astype(vbt_ref.dtype)
    for c in range(tm // KCH):
        vbt_ref[c] = vbt[:, c * KCH:(c + 1) * KCH]

    gdim = SGU_WIDTH // SGU_GROUPS
    for ci in range(tm // SGU_CHUNK):
        rows = slice(ci * SGU_CHUNK, (ci + 1) * SGU_CHUNK)
        for gi in range(SGU_GROUPS):
            cols = slice(gi * gdim, (gi + 1) * gdim)
            s = _dot(sgu_w_ref[gi], vn[rows, cols]) + sgu_b_ref[:, cols]
            oc_ref[rows, cols] = (u[rows, cols] * s).astype(oc_ref.dtype)


def _inproj(x2, g, wp, wt, tables, sgu_g, sgu_w, sgu_b):
    n, d = x2.shape
    cos, sin, cost, sint = tables
    row = lambda w: pl.BlockSpec((TM, w), lambda i: (i, 0))
    colt = lambda r: pl.BlockSpec((r, TM), lambda i: (0, i))
    chunk = lambda r: pl.BlockSpec((TM // KCH, r, KCH), lambda i: (i, 0, 0))
    bf = MXU_DTYPE
    outs = [
        (row(256), jax.ShapeDtypeStruct((n, 256), jnp.float32)),
        (row(LANE), jax.ShapeDtypeStruct((n, LANE), bf)),
        (row(LANE), jax.ShapeDtypeStruct((n, LANE), bf)),
        (row(512), jax.ShapeDtypeStruct((n, 512), bf)),
        (row(SGU_WIDTH), jax.ShapeDtypeStruct((n, SGU_WIDTH), bf)),
        (colt(512), jax.ShapeDtypeStruct((512, n), bf)),
        (colt(512), jax.ShapeDtypeStruct((512, n), bf)),
        (chunk(LANE), jax.ShapeDtypeStruct((n // KCH, LANE, KCH), bf)),
        (chunk(LANE), jax.ShapeDtypeStruct((n // KCH, LANE, KCH), bf)),
        (colt(G_ROWS), jax.ShapeDtypeStruct((G_ROWS, n), jnp.float32)),
        (colt(512), jax.ShapeDtypeStruct((512, n), bf)),
        (chunk(512), jax.ShapeDtypeStruct((n // KCH, 512, KCH), bf)),
    ]
    return pl.pallas_call(
        _inproj_kernel,
        grid=(n // TM,),
        in_specs=[row(d), _const_spec((1, d)), _const_spec(wp.shape), _const_spec(wt.shape),
                  row(LANE), row(LANE), colt(HALF), colt(HALF),
                  _const_spec(sgu_g.shape), _const_spec(sgu_w.shape), _const_spec(sgu_b.shape)],
        out_specs=[o[0] for o in outs],
        out_shape=[o[1] for o in outs],
        compiler_params=_params("parallel"),
        name="inproj",
    )(x2, g, wp, wt, cos, sin, cost, sint, sgu_g, sgu_w, sgu_b)


def _compress_k_kernel(x_ref, pa_ref, pb_ref, w1a_ref, w1b_ref, w2_ref, o_ref):
    nc = x_ref.shape[1]
    acc = jnp.zeros(o_ref.shape, jnp.float32)
    for g in range(NSA_GROUPS):
        xg = x_ref[g]
        a = _dot((xg + pa_ref[...]).astype(MXU_DTYPE), w1a_ref[...])
        b = _dot((xg + pb_ref[...]).astype(MXU_DTYPE), w1b_ref[...])
        hid = jax.nn.gelu(a + pltpu.roll(b, nc - 1, 0))
        acc = acc + _dot(hid.astype(MXU_DTYPE), w2_ref[g])
    o_ref[...] = acc.astype(o_ref.dtype)


def _compress_v_kernel(x_ref, pa_ref, pb_ref, w1at_ref, w1bt_ref, w2t_ref, o_ref):
    nc = x_ref.shape[1]
    acc = jnp.zeros(o_ref.shape, jnp.float32)
    for g in range(NSA_GROUPS):
        xg = x_ref[g]
        a = _dot_nt(w1at_ref[...], (xg + pa_ref[...]).astype(MXU_DTYPE))
        b = _dot_nt(w1bt_ref[...], (xg + pb_ref[...]).astype(MXU_DTYPE))
        hid = jax.nn.gelu(a + pltpu.roll(b, nc - 1, 1))
        acc = acc + _dot(w2t_ref[g], hid.astype(MXU_DTYPE))
    o_ref[...] = acc.astype(o_ref.dtype)


def _compress(xc, kind, pos, w1, w2, transposed):
    b, _, nc, cw = xc.shape
    pos_flat = pos.reshape(1, CMP_BLOCK * HEAD_DIM)
    pa, pb = pos_flat[:, :cw], pos_flat[:, cw:]
    w1a, w1b = w1[:cw].astype(MXU_DTYPE), w1[cw:].astype(MXU_DTYPE)
    w2p = jnp.zeros((NSA_GROUPS, CMP_HIDDEN, LANE), jnp.float32)
    for g in range(NSA_GROUPS):
        w2p = w2p.at[g, :, g * HEAD_DIM:(g + 1) * HEAD_DIM].set(w2)
    w2p = w2p.astype(MXU_DTYPE)
    x_spec = pl.BlockSpec((None, NSA_GROUPS, nc, cw), lambda i: (i, kind, 0, 0))
    if not transposed:
        return pl.pallas_call(
            _compress_k_kernel, grid=(b,),
            in_specs=[x_spec, _const_spec(pa.shape), _const_spec(pb.shape), _const_spec(w1a.shape),
                      _const_spec(w1b.shape), _const_spec(w2p.shape)],
            out_specs=pl.BlockSpec((None, nc, LANE), lambda i: (i, 0, 0)),
            out_shape=jax.ShapeDtypeStruct((b, nc, LANE), MXU_DTYPE),
            compiler_params=_params("parallel"), name="compress_k",
        )(xc, pa, pb, w1a, w1b, w2p)
    w1at, w1bt, w2pt = w1a.T, w1b.T, jnp.swapaxes(w2p, 1, 2)
    return pl.pallas_call(
        _compress_v_kernel, grid=(b,),
        in_specs=[x_spec, _const_spec(pa.shape), _const_spec(pb.shape), _const_spec(w1at.shape),
                  _const_spec(w1bt.shape), _const_spec(w2pt.shape)],
        out_specs=pl.BlockSpec((None, LANE, nc), lambda i: (i, 0, 0)),
        out_shape=jax.ShapeDtypeStruct((b, LANE, nc), MXU_DTYPE),
        compiler_params=_params("parallel"), name="compress_v",
    )(xc, pa, pb, w1at, w1bt, w2pt)


def _group_queries_t(qt):
    z = jnp.zeros((HEAD_DIM, qt.shape[1]), qt.dtype)
    cols = []
    for h in range(NSA_HEADS):
        slab = qt[h * HEAD_DIM:(h + 1) * HEAD_DIM]
        cols.append(jnp.concatenate([slab, z] if h < HPG else [z, slab], axis=0))
    return jnp.concatenate(cols, axis=1)


def _with_ones_rows(vt):
    return jnp.concatenate([vt, jnp.ones((ONES_ROWS, vt.shape[1]), vt.dtype)], axis=0)


def _normalized(acc):
    dims = acc.shape[0] - ONES_ROWS
    return acc[:dims] * (1.0 / acc[dims:dims + 1])


def _pipelined_chains(n_chain, qk, softmax, pv):
    outs = []
    s_next = qk(0)
    pending = None
    for c in range(n_chain):
        s = s_next
        if c + 1 < n_chain:
            s_next = qk(c + 1)
        r = softmax(c, s)
        if pending is not None:
            outs.append(pv(*pending))
        pending = (c, r)
    outs.append(pv(*pending))
    return outs


def _nsa_kernel(qat_ref, qart_ref, gt_ref, kc_ref, vct_ref, kslc_ref, vslct_ref, kwin_ref, vwint_ref,
                ovt_ref, e_ref, o_ref, m_ref, acc_ref, sa_ref, sb_ref):
    qi = pl.program_id(1)
    tq, tk = NSA_TQ, NSA_TK
    nc = kc_ref.shape[0]
    n_chain = NSA_HEADS * tq // CHAIN
    hpc = CHAIN // tq
    t0 = qi * tq
    tok = t0 + lax.broadcasted_iota(jnp.int32, (1, tq), 1)
    tok_c = jnp.concatenate([tok] * hpc, axis=1)
    qc = _group_queries_t(qat_ref[...])
    qr = _group_queries_t(qart_ref[...])
    chain_cols = [slice(c * CHAIN, (c + 1) * CHAIN) for c in range(n_chain)]
    chain_group = [(c * hpc) // HPG for c in range(n_chain)]

    cmp_end = lax.broadcasted_iota(jnp.int32, (nc, 1), 0) * CMP_STRIDE + (CMP_BLOCK - 1)
    mask_c = cmp_end <= tok_c
    has_key = tok_c >= CMP_BLOCK - 1
    vo = jnp.concatenate([_with_ones_rows(vct_ref[...]), ovt_ref[...]], axis=0)
    imp_g = [None] * NSA_GROUPS

    def cmp_softmax(c, s):
        s = jnp.where(mask_c, s, NEG)
        return jnp.exp2(s - jnp.max(s, axis=0, keepdims=True)).astype(MXU_DTYPE)

    def cmp_pv(c, p):
        r = _dot(vo, p)
        inv = jnp.where(has_key, 1.0 / r[LANE:LANE + 1], 0.0)
        imp_c = r[LANE + ONES_ROWS:] * inv
        for i in range(hpc):
            g = chain_group[c]
            part = imp_c[:, i * tq:(i + 1) * tq]
            imp_g[g] = part if imp_g[g] is None else imp_g[g] + part
        return r[:LANE] * inv

    o_cmp = jnp.concatenate(_pipelined_chains(
        n_chain, lambda c: _dot(kc_ref[...], qc[:, chain_cols[c]]), cmp_softmax, cmp_pv), axis=1)
    imp = jnp.concatenate(imp_g, axis=1)

    tok2 = jnp.concatenate([tok] * NSA_GROUPS, axis=1)
    blk = lax.broadcasted_iota(jnp.int32, (NSP, 1), 0)
    blk_f = blk.astype(jnp.float32)
    cur = tok2 // SLC_BLOCK
    forced = (blk == 0) | (blk == cur) | (blk == cur - 1)
    score = jnp.where(forced, BIG, jnp.where(blk * SLC_BLOCK <= tok2, imp, -BIG))
    bias = jnp.full(score.shape, NEG, jnp.float32)
    for _ in range(SLC_TOPK):
        best = jnp.max(score, axis=0, keepdims=True)
        first = jnp.min(jnp.where(score == best, blk_f, float(NSP)), axis=0, keepdims=True)
        pick = blk_f == first
        bias = jnp.where(pick, 0.0, bias)
        score = jnp.where(pick, REMOVED, score)
    bias = bias.astype(MXU_DTYPE)

    qa = []
    for c in range(n_chain):
        g = chain_group[c]
        bias_c = jnp.concatenate([bias[:, g * tq:(g + 1) * tq]] * hpc, axis=1)
        qa.append(jnp.concatenate([qr[:, chain_cols[c]], bias_c], axis=0))
    m_ref[...] = jnp.full(m_ref.shape, NEG, jnp.float32)
    acc_ref[...] = jnp.zeros(acc_ref.shape, jnp.float32)
    cpt = tk // KCH

    def slc_qk(j, c):
        k0 = pl.multiple_of(j * tk, tk)
        ka = jnp.concatenate([kslc_ref[pl.ds(k0, tk), :], e_ref[pl.ds(k0, tk), :]], axis=1)
        return _dot(ka, qa[c])

    def slc_tile(j, cur_ref, nxt_ref, masked):
        vt = _with_ones_rows(jnp.concatenate([vslct_ref[j * cpt + c] for c in range(cpt)], axis=1))
        kpos = j * tk + lax.broadcasted_iota(jnp.int32, (tk, 1), 0)
        m_all, acc_all = m_ref[...], acc_ref[...]
        m_out, acc_out = [], []
        pending = None
        for c in range(n_chain):
            cols = chain_cols[c]
            if nxt_ref is not None:
                nxt_ref[c] = slc_qk(jnp.maximum(j - 1, 0), c)
            s = cur_ref[c]
            if masked:
                s = jnp.where(kpos <= tok_c, s, NEG)
            m_old = m_all[:, cols]
            m_new = jnp.maximum(m_old, jnp.max(s, axis=0, keepdims=True))
            alpha = jnp.exp2(m_old - m_new)
            p = jnp.exp2(s - m_new)
            m_out.append(m_new)
            if pending is not None:
                acc_out.append(pending[0] + _dot(vt, pending[1]))
            pending = (acc_all[:, cols] * alpha, p.astype(MXU_DTYPE))
        acc_out.append(pending[0] + _dot(vt, pending[1]))
        m_ref[...] = jnp.concatenate(m_out, axis=1)
        acc_ref[...] = jnp.concatenate(acc_out, axis=1)

    last = t0 // tk
    for c in range(n_chain):
        sa_ref[c] = slc_qk(last, c)
    slc_tile(last, sa_ref, sb_ref, True)

    def pair(i, carry):
        j = last - 1 - 2 * i
        slc_tile(j, sb_ref, sa_ref, False)
        slc_tile(j - 1, sa_ref, sb_ref, False)
        return carry

    lax.fori_loop(0, last // 2, pair, 0)

    @pl.when(last % 2 == 1)
    def _():
        slc_tile(0, sb_ref, None, False)

    o_slc = _normalized(acc_ref[...])

    nwc = (WINDOW + tq) // KCH
    c0 = jnp.maximum(qi * (tq // KCH) - WINDOW // KCH, 0)
    w0 = pl.multiple_of(c0 * KCH, KCH)
    kpos = w0 + lax.broadcasted_iota(jnp.int32, (WINDOW + tq, 1), 0)
    valid_w = (kpos <= tok_c) & (kpos > tok_c - WINDOW)
    vt_w = _with_ones_rows(jnp.concatenate([vwint_ref[c0 + c] for c in range(nwc)], axis=1))

    def win_softmax(c, s):
        s = jnp.where(valid_w, s, NEG)
        return jnp.exp2(s - jnp.max(s, axis=0, keepdims=True)).astype(MXU_DTYPE)

    o_win = jnp.concatenate(_pipelined_chains(
        n_chain, lambda c: _dot(kwin_ref[pl.ds(w0, WINDOW + tq), :], qr[:, chain_cols[c]]), win_softmax,
        lambda c, p: _normalized(_dot(vt_w, p))), axis=1)

    gt = gt_ref[...]
    outs = []
    for h in range(NSA_HEADS):
        rows = slice((h // HPG) * HEAD_DIM, (h // HPG + 1) * HEAD_DIM)
        cols = slice(h * tq, (h + 1) * tq)
        outs.append(gt[3 * h:3 * h + 1] * o_cmp[rows, cols] + gt[3 * h + 1:3 * h + 2] * o_slc[rows, cols]
                    + gt[3 * h + 2:3 * h + 3] * o_win[rows, cols])
    o_ref[...] = jnp.concatenate(outs, axis=0).T.astype(o_ref.dtype)


def _nsa(bsz, seq, qat, qart, gt, kc, vct, kslc, vslct, kwin, vwint, ovt, onehot):
    n = bsz * seq
    nq = seq // NSA_TQ
    nc = kc.shape[1]
    n_chain = NSA_HEADS * NSA_TQ // CHAIN
    qspec = lambda r: pl.BlockSpec((r, NSA_TQ), lambda b, i: (0, b * nq + i))
    kspec = pl.BlockSpec((seq, LANE), lambda b, i: (b, 0))
    vspec = pl.BlockSpec((seq // KCH, LANE, KCH), lambda b, i: (b, 0, 0))
    return pl.pallas_call(
        _nsa_kernel,
        grid=(bsz, nq),
        in_specs=[qspec(512), qspec(512), qspec(G_ROWS),
                  pl.BlockSpec((None, nc, LANE), lambda b, i: (b, 0, 0)),
                  pl.BlockSpec((None, LANE, nc), lambda b, i: (b, 0, 0)),
                  kspec, vspec, kspec, vspec, _const_spec(ovt.shape), _const_spec(onehot.shape)],
        out_specs=pl.BlockSpec((NSA_TQ, 512), lambda b, i: (b * nq + i, 0)),
        out_shape=jax.ShapeDtypeStruct((n, 512), MXU_DTYPE),
        scratch_shapes=[pltpu.VMEM((1, NSA_HEADS * NSA_TQ), jnp.float32),
                        pltpu.VMEM((LANE + ONES_ROWS, NSA_HEADS * NSA_TQ), jnp.float32),
                        pltpu.VMEM((n_chain, NSA_TK, CHAIN), jnp.float32),
                        pltpu.VMEM((n_chain, NSA_TK, CHAIN), jnp.float32)],
        compiler_params=_params("parallel", "parallel"),
        name="nsa_attention",
    )(qat, qart, gt, kc, vct, kslc, vslct, kwin, vwint, ovt, onehot)


def _diff_kernel(lambda_init, q_ref, k_ref, vt_ref, lq1_ref, lk1_ref, lq2_ref, lk2_ref, subln_ref,
                 o_ref, m_ref, acc_ref, sa_ref, sb_ref):
    qi = pl.program_id(2)
    tq, tk = DIFF_TQ, DIFF_TK
    cpt = tk // KCH
    t0 = qi * tq
    qt = q_ref[...]
    z = jnp.zeros((HEAD_DIM, tq), qt.dtype)
    qp = jnp.concatenate([jnp.concatenate([qt[:HEAD_DIM], z], axis=0),
                          jnp.concatenate([z, qt[HEAD_DIM:]], axis=0)], axis=1)
    tok = t0 + lax.broadcasted_iota(jnp.int32, (1, CHAIN), 1)

    m_ref[...] = jnp.full(m_ref.shape, NEG, jnp.float32)
    acc_ref[...] = jnp.zeros(acc_ref.shape, jnp.float32)

    n_chain = 2 * tq // CHAIN

    def qk(j, c):
        k = k_ref[pl.ds(pl.multiple_of(j * tk, tk), tk), :]
        return _dot(k, qp[:, c * CHAIN:(c + 1) * CHAIN])

    def tile(j, cur_ref, nxt_ref, masked):
        vt = _with_ones_rows(jnp.concatenate([vt_ref[j * cpt + c] for c in range(cpt)], axis=1))
        kpos = j * tk + lax.broadcasted_iota(jnp.int32, (tk, 1), 0)
        m_all, acc_all = m_ref[...], acc_ref[...]
        m_out, acc_out = [], []
        pending = None
        for c in range(n_chain):
            cols = slice(c * CHAIN, (c + 1) * CHAIN)
            if nxt_ref is not None:
                nxt_ref[c] = qk(jnp.maximum(j - 1, 0), c)
            s = cur_ref[c]
            if masked:
                s = jnp.where(kpos <= tok + (c * CHAIN) % tq, s, NEG)
            m_old = m_all[:, cols]
            m_new = jnp.maximum(m_old, jnp.max(s, axis=0, keepdims=True))
            alpha = jnp.exp2(m_old - m_new)
            p = jnp.exp2(s - m_new)
            m_out.append(m_new)
            if pending is not None:
                acc_out.append(pending[0] + _dot(vt, pending[1]))
            pending = (acc_all[:, cols] * alpha, p.astype(MXU_DTYPE))
        acc_out.append(pending[0] + _dot(vt, pending[1]))
        m_ref[...] = jnp.concatenate(m_out, axis=1)
        acc_ref[...] = jnp.concatenate(acc_out, axis=1)

    last = t0 // tk
    for c in range(n_chain):
        sa_ref[c] = qk(last, c)
    tile(last, sa_ref, sb_ref, True)

    def pair(i, carry):
        j = last - 1 - 2 * i
        tile(j, sb_ref, sa_ref, False)
        tile(j - 1, sa_ref, sb_ref, False)
        return carry

    lax.fori_loop(0, last // 2, pair, 0)

    @pl.when(last % 2 == 1)
    def _():
        tile(0, sb_ref, None, False)

    lam = (jnp.exp(jnp.sum(lq1_ref[...] * lk1_ref[...], axis=1, keepdims=True))
           - jnp.exp(jnp.sum(lq2_ref[...] * lk2_ref[...], axis=1, keepdims=True)) + lambda_init)
    o = _normalized(acc_ref[...])
    o = o[:, :tq] - lam * o[:, tq:]
    o = o * lax.rsqrt(jnp.mean(o * o, axis=0, keepdims=True) + NORM_EPS)
    o = o * subln_ref[...] * (1.0 - lambda_init)
    o_ref[...] = o.T.astype(o_ref.dtype)


def _diff(bsz, seq, lambda_init, qbt, kb, vbt, lq1, lk1, lq2, lk2, subln):
    n = bsz * seq
    nq = seq // DIFF_TQ
    vec = _const_spec((1, HEAD_DIM))
    return pl.pallas_call(
        functools.partial(_diff_kernel, lambda_init),
        grid=(bsz, DIFF_HEADS, nq),
        in_specs=[pl.BlockSpec((DIFF_V_DIM, DIFF_TQ), lambda b, h, i: (h, b * nq + i)),
                  pl.BlockSpec((seq, LANE), lambda b, h, i: (b, h)),
                  pl.BlockSpec((seq // KCH, DIFF_V_DIM, KCH), lambda b, h, i: (b, h, 0)),
                  vec, vec, vec, vec, _const_spec((DIFF_V_DIM, 1))],
        out_specs=pl.BlockSpec((DIFF_TQ, DIFF_V_DIM), lambda b, h, i: (b * nq + i, h)),
        out_shape=jax.ShapeDtypeStruct((n, DIFF_HEADS * DIFF_V_DIM), MXU_DTYPE),
        scratch_shapes=[pltpu.VMEM((1, 2 * DIFF_TQ), jnp.float32),
                        pltpu.VMEM((DIFF_V_DIM + ONES_ROWS, 2 * DIFF_TQ), jnp.float32),
                        pltpu.VMEM((2 * DIFF_TQ // CHAIN, DIFF_TK, CHAIN), jnp.float32),
                        pltpu.VMEM((2 * DIFF_TQ // CHAIN, DIFF_TK, CHAIN), jnp.float32)],
        compiler_params=_params("parallel", "parallel", "parallel"),
        name="diff_attention",
    )(qbt, kb, vbt, lq1, lk1, lq2, lk2, subln)


def _merge_kernel(x_ref, g_ref, wm_ref, bm_ref, oa_ref, ob_ref, oc_ref, wa_ref, wb_ref, wc_ref, wo_ref, o_ref):
    x = x_ref[...]
    d = x.shape[1]
    hb = _rmsnorm_rows(x, g_ref[...]).astype(MXU_DTYPE)
    mixed = None
    for i, (br_ref, w_ref) in enumerate(((oa_ref, wa_ref), (ob_ref, wb_ref), (oc_ref, wc_ref))):
        cols = slice(i * d, (i + 1) * d)
        gate = jax.nn.sigmoid(_dot(hb, wm_ref[:, cols]) + bm_ref[:, cols])
        term = gate * _dot(br_ref[...], w_ref[...])
        mixed = term if mixed is None else mixed + term
    o_ref[...] = x + _dot(mixed.astype(MXU_DTYPE), wo_ref[...])


def _merge(x2, g, wm, bm, oa, ob, oc, wa, wb, wc, wo):
    n, d = x2.shape
    row = lambda w: pl.BlockSpec((TM, w), lambda i: (i, 0))
    return pl.pallas_call(
        _merge_kernel,
        grid=(n // TM,),
        in_specs=[row(d), _const_spec(g.shape), _const_spec(wm.shape), _const_spec(bm.shape),
                  row(BRANCH), row(BRANCH), row(BRANCH),
                  _const_spec(wa.shape), _const_spec(wb.shape), _const_spec(wc.shape), _const_spec(wo.shape)],
        out_specs=row(d),
        out_shape=jax.ShapeDtypeStruct((n, d), jnp.float32),
        compiler_params=_params("parallel"),
        name="merge_out",
    )(x2, g, wm, bm, oa, ob, oc, wa, wb, wc, wo)


def _ffn_kernel(final, n_chunks, x_ref, g_ref, w1_ref, w3_ref, w2_ref, gf_ref, o_ref):
    x = x_ref[...]
    hb = _rmsnorm_rows(x, g_ref[...]).astype(MXU_DTYPE)
    cw = w1_ref.shape[1] // n_chunks
    y = x
    for c in range(n_chunks):
        cols = slice(c * cw, (c + 1) * cw)
        act = jax.nn.silu(_dot(hb, w1_ref[:, cols])) * _dot(hb, w3_ref[:, cols])
        y = y + _dot(act.astype(MXU_DTYPE), w2_ref[cols, :])
    if final:
        y = _rmsnorm_rows(y, gf_ref[...])
    o_ref[...] = y


def _ffn(x2, g, w1, w3, w2, gf, final):
    n, d = x2.shape
    dff = w1.shape[1]
    n_chunks = 2 if dff % (2 * LANE) == 0 else 1
    row = pl.BlockSpec((TM, d), lambda i: (i, 0))
    return pl.pallas_call(
        functools.partial(_ffn_kernel, final, n_chunks),
        grid=(n // TM,),
        in_specs=[row, _const_spec(g.shape), _const_spec(w1.shape), _const_spec(w3.shape),
                  _const_spec(w2.shape), _const_spec(gf.shape)],
        out_specs=row,
        out_shape=jax.ShapeDtypeStruct((n, d), jnp.float32),
        compiler_params=_params("parallel"),
        name="ffn",
    )(x2, g, w1, w3, w2, gf)


def _overlap_t(seq):
    nc = seq // CMP_STRIDE
    c_start = jnp.arange(nc)[None, :] * CMP_STRIDE
    s_start = jnp.arange(NSP)[:, None] * SLC_BLOCK
    ov = (c_start < s_start + SLC_BLOCK) & (c_start + CMP_BLOCK > s_start)
    return ov.astype(MXU_DTYPE)


def _block_onehot(seq):
    return (jnp.arange(seq)[:, None] // SLC_BLOCK == jnp.arange(NSP)[None, :]).astype(MXU_DTYPE)


def kernel(x, positions, attn_norm, w_in, cmp_pos_k, cmp_k_w1, cmp_k_w2, cmp_pos_v, cmp_v_w1, cmp_v_w2, diff_lq1, diff_lk1, diff_lq2, diff_lk2, diff_subln, sgu_norm, sgu_w, sgu_b, w_branch_a, w_branch_b, w_branch_c, w_merge, b_merge, w_out, ffn_norm, w_ffn1, w_ffn3, w_ffn2, final_norm):
    bsz, seq, d = x.shape
    depth = w_in.shape[0]
    n = bsz * seq
    assert seq % NSA_TK == 0 and seq >= WINDOW + NSA_TQ and seq // SLC_BLOCK <= NSP and n % TM == 0
    bf = MXU_DTYPE
    nc = seq // CMP_STRIDE
    cw = CMP_STRIDE * HEAD_DIM

    tables = _rope_tables(positions)
    ovt = _overlap_t(seq)
    onehot = _block_onehot(seq)
    causal = jnp.tril(jnp.ones((SGU_CHUNK, SGU_CHUNK), dtype=bool))
    x2 = x.reshape(n, d)
    row = lambda v: v.reshape(1, -1)

    for l in range(depth):
        lambda_init = 0.8 - 0.6 * math.exp(-0.3 * l)
        w = w_in[l]
        wp = jnp.concatenate([w[:, 512:768], w[:, 768:896], w[:, 1024:1152], w[:, 1816:2328], w[:, 2840:3864]],
                             axis=1).astype(bf)
        wt = jnp.concatenate([w[:, 0:512], w[:, 896:1024], w[:, 1152:1280], w[:, 1280:1304],
                              jnp.zeros((d, G_ROWS - 3 * NSA_HEADS), w.dtype), w[:, 1304:1816], w[:, 2328:2840]],
                             axis=1).T.astype(bf)
        sgu_wm = jnp.where(causal[None], sgu_w[l], 0.0).astype(bf)
        sgu_bias = jnp.repeat(sgu_b[l].T, SGU_WIDTH // SGU_GROUPS, axis=1)

        (kvc, kslc, kwin, kb, oc, qat, qart, vslct, vwint, gt, qbt, vbt) = _inproj(
            x2, row(attn_norm[l]), wp, wt, tables, row(sgu_norm[l]), sgu_wm, sgu_bias)

        xc = kvc.reshape(bsz, seq, 2 * NSA_GROUPS, HEAD_DIM).transpose(0, 2, 1, 3).reshape(
            bsz, 2 * NSA_GROUPS, nc, cw)
        kc = _compress(xc, 0, cmp_pos_k[l], cmp_k_w1[l], cmp_k_w2[l], transposed=False)
        vct = _compress(xc, 1, cmp_pos_v[l], cmp_v_w1[l], cmp_v_w2[l], transposed=True)

        oa = _nsa(bsz, seq, qat, qart, gt, kc, vct, kslc, vslct, kwin, vwint, ovt, onehot)
        ob = _diff(bsz, seq, lambda_init, qbt, kb, vbt, row(diff_lq1[l]), row(diff_lk1[l]),
                   row(diff_lq2[l]), row(diff_lk2[l]), diff_subln[l].reshape(DIFF_V_DIM, 1))

        x2 = _merge(x2, row(attn_norm[l]), w_merge[l].astype(bf), row(b_merge[l]), oa, ob, oc,
                    w_branch_a[l].astype(bf), w_branch_b[l].astype(bf), w_branch_c[l].astype(bf),
                    w_out[l].astype(bf))
        x2 = _ffn(x2, row(ffn_norm[l]), w_ffn1[l].astype(bf), w_ffn3[l].astype(bf), w_ffn2[l].astype(bf),
                  row(final_norm), final=(l == depth - 1))
    return x2.reshape(bsz, seq, d)
```

```python
import functools
import math

import jax
import jax.numpy as jnp
from jax import lax
from jax.experimental import pallas as pl
from jax.experimental.pallas import tpu as pltpu

HEAD_DIM = 64
HALF = HEAD_DIM // 2
ROPE_THETA = 10000.0
NORM_EPS = 1e-6
BIG = 1e9
NEG = -1e30
REMOVED = -3e38

NSA_HEADS = 8
NSA_GROUPS = 2
HPG = NSA_HEADS // NSA_GROUPS
CMP_BLOCK = 32
CMP_STRIDE = 16
CMP_HIDDEN = 256
SLC_BLOCK = 64
SLC_TOPK = 16
WINDOW = 512
DIFF_HEADS = 4
DIFF_V_DIM = 2 * HEAD_DIM
SGU_CHUNK = 128
SGU_GROUPS = 4
SGU_WIDTH = 512
BRANCH = 512
SCALE = HEAD_DIM ** -0.5
QSCALE = SCALE * math.log2(math.e)
ONES_ROWS = 16

LANE = 128
NSP = 128
MXU_DTYPE = jnp.bfloat16
VMEM_LIMIT = 56 * 1024 * 1024

TM = 512
NSA_TQ = 256
NSA_TK = 512
DIFF_TQ = 512
DIFF_TK = 512
CHAIN = 256
CAUSAL_VARIANTS = 4
KCH = 128


def _params(*sem):
    return pltpu.CompilerParams(dimension_semantics=sem, vmem_limit_bytes=VMEM_LIMIT)


def _const_spec(shape):
    nd = len(shape)
    return pl.BlockSpec(shape, lambda *_: (0,) * nd, pipeline_mode=pl.Buffered(1))


def _dot(a, b):
    return jnp.dot(a, b, preferred_element_type=jnp.float32)


def _dot_nt(a, b):
    return lax.dot_general(a, b, (((1,), (1,)), ((), ())), preferred_element_type=jnp.float32)


def _rmsnorm_rows(x, g):
    return x * lax.rsqrt(jnp.mean(x * x, axis=-1, keepdims=True) + NORM_EPS) * g


def _tables_kernel(pos_row_ref, f_col_ref, cos_ref, sin_ref, cost_ref, sint_ref):
    angt = f_col_ref[...] * pos_row_ref[...].astype(jnp.float32)
    cost, sint = jnp.cos(angt), jnp.sin(angt)
    cost_ref[...] = cost
    sint_ref[...] = sint
    reps = LANE // HALF
    cos_ref[...] = jnp.concatenate([cost] * reps, axis=0).T
    sin = jnp.concatenate([sint] * reps, axis=0).T
    lane = lax.broadcasted_iota(jnp.int32, sin.shape, 1)
    sin_ref[...] = jnp.where((lane % HEAD_DIM) < HALF, -sin, sin)


def _rope_tables(positions):
    n = positions.size
    inv_freq = ROPE_THETA ** (-jnp.arange(HALF, dtype=jnp.float32) / HALF)
    f_col = inv_freq.reshape(HALF, 1)
    pos_row = positions.reshape(1, n)
    return pl.pallas_call(
        _tables_kernel,
        grid=(n // TM,),
        in_specs=[pl.BlockSpec((1, TM), lambda i: (0, i)), _const_spec((HALF, 1))],
        out_specs=[pl.BlockSpec((TM, LANE), lambda i: (i, 0)),
                   pl.BlockSpec((TM, LANE), lambda i: (i, 0)),
                   pl.BlockSpec((HALF, TM), lambda i: (0, i)),
                   pl.BlockSpec((HALF, TM), lambda i: (0, i))],
        out_shape=[jax.ShapeDtypeStruct((n, LANE), jnp.float32),
                   jax.ShapeDtypeStruct((n, LANE), jnp.float32),
                   jax.ShapeDtypeStruct((HALF, n), jnp.float32),
                   jax.ShapeDtypeStruct((HALF, n), jnp.float32)],
        compiler_params=_params("parallel"),
        name="rope_tables",
    )(pos_row, f_col)


P_KVC, P_KSLC, P_KWIN, P_KB, P_UV, P_END = 0, 256, 384, 512, 1024, 2048
T_QA, T_VSLC, T_VWIN, T_G, T_QB, T_VB, T_END = 0, 512, 640, 768, 800, 1312, 1824
G_ROWS = T_QB - T_G


def _rope_plain(y, cos, sin):
    lane = lax.broadcasted_iota(jnp.int32, y.shape, 1)
    swapped = jnp.where((lane % HEAD_DIM) < HALF, pltpu.roll(y, LANE - HALF, 1), pltpu.roll(y, HALF, 1))
    return y * cos + swapped * sin


def _rope_t(y, cos, sin):
    y1, y2 = y[:HALF], y[HALF:]
    return jnp.concatenate([y1 * cos - y2 * sin, y2 * cos + y1 * sin], axis=0)


def _inproj_kernel(x_ref, g_ref, wp_ref, wt_ref, cos_ref, sin_ref, cost_ref, sint_ref,
                   sgu_g_ref, sgu_w_ref, sgu_b_ref,
                   kcmp_ref, vcmp_ref, kslc_ref, kwin_ref, kb_ref, oc_ref,
                   qat_ref, qart_ref, vslct_ref, vwint_ref, gt_ref, qbt_ref, vbt_ref):
    x = x_ref[...]
    hb = _rmsnorm_rows(x, g_ref[...]).astype(MXU_DTYPE)
    cos, sin = cos_ref[...], sin_ref[...]
    cost, sint = cost_ref[...], sint_ref[...]
    tm = x.shape[0]

    z = jax.nn.gelu(_dot(hb, wp_ref[:, P_UV:P_END]))
    u = z[:, :SGU_WIDTH]
    vn = _rmsnorm_rows(z[:, SGU_WIDTH:], sgu_g_ref[...]).astype(MXU_DTYPE)

    kvc = _dot(hb, wp_ref[:, P_KVC:P_KSLC])
    kcmp_ref[...] = kvc[:, :LANE]
    vcmp_ref[...] = kvc[:, LANE:]
    kslc_ref[...] = _rope_plain(_dot(hb, wp_ref[:, P_KSLC:P_KWIN]), cos, sin).astype(kslc_ref.dtype)
    kwin_ref[...] = _rope_plain(_dot(hb, wp_ref[:, P_KWIN:P_KB]), cos, sin).astype(kwin_ref.dtype)
    kb = _dot(hb, wp_ref[:, P_KB:P_UV])
    for c in range((P_UV - P_KB) // LANE):
        sl = slice(c * LANE, (c + 1) * LANE)
        kb_ref[:, sl] = _rope_plain(kb[:, sl], cos, sin).astype(kb_ref.dtype)

    qat = _dot_nt(wt_ref[T_QA:T_VSLC, :], hb) * QSCALE
    qat_ref[...] = qat.astype(qat_ref.dtype)
    for h in range(NSA_HEADS):
        sl = slice(h * HEAD_DIM, (h + 1) * HEAD_DIM)
        qart_ref[sl, :] = _rope_t(qat[sl], cost, sint).astype(qart_ref.dtype)
    vst = _dot_nt(wt_ref[T_VSLC:T_VWIN, :], hb).astype(vslct_ref.dtype)
    vwt = _dot_nt(wt_ref[T_VWIN:T_G, :], hb).astype(vwint_ref.dtype)
    for c in range(tm // KCH):
        sl = slice(c * KCH, (c + 1) * KCH)
        vslct_ref[c] = vst[:, sl]
        vwint_ref[c] = vwt[:, sl]
    gt_ref[...] = jax.nn.sigmoid(_dot_nt(wt_ref[T_G:T_QB, :], hb))
    qbt = _dot_nt(wt_ref[T_QB:T_VB, :], hb) * QSCALE
    for h in range(2 * DIFF_HEADS):
        sl = slice(h * HEAD_DIM, (h + 1) * HEAD_DIM)
        qbt_ref[sl, :] = _rope_t(qbt[sl], cost, sint).astype(qbt_ref.dtype)
    vbt = _dot_nt(wt_ref[T_VB:T_END, :], hb).astype(vbt_ref.dtype)
    for c in range(tm // KCH):
        vbt_ref[c] = vbt[:, c * KCH:(c + 1) * KCH]

    gdim = SGU_WIDTH // SGU_GROUPS
    for ci in range(tm // SGU_CHUNK):
        rows = slice(ci * SGU_CHUNK, (ci + 1) * SGU_CHUNK)
        for gi in range(SGU_GROUPS):
            cols = slice(gi * gdim, (gi + 1) * gdim)
            s = _dot(sgu_w_ref[gi], vn[rows, cols]) + sgu_b_ref[:, cols]
            oc_ref[rows, cols] = (u[rows, cols] * s).astype(oc_ref.dtype)


def _inproj(x2, g, wp, wt, tables, sgu_g, sgu_w, sgu_b):
    n, d = x2.shape
    cos, sin, cost, sint = tables
    row = lambda w: pl.BlockSpec((TM, w), lambda i: (i, 0))
    colt = lambda r: pl.BlockSpec((r, TM), lambda i: (0, i))
    chunk = lambda r: pl.BlockSpec((TM // KCH, r, KCH), lambda i: (i, 0, 0))
    bf = MXU_DTYPE
    outs = [
        (row(LANE), jax.ShapeDtypeStruct((n, LANE), jnp.float32)),
        (row(LANE), jax.ShapeDtypeStruct((n, LANE), jnp.float32)),
        (row(LANE), jax.ShapeDtypeStruct((n, LANE), bf)),
        (row(LANE), jax.ShapeDtypeStruct((n, LANE), bf)),
        (row(512), jax.ShapeDtypeStruct((n, 512), bf)),
        (row(SGU_WIDTH), jax.ShapeDtypeStruct((n, SGU_WIDTH), bf)),
        (colt(512), jax.ShapeDtypeStruct((512, n), bf)),
        (colt(512), jax.ShapeDtypeStruct((512, n), bf)),
        (chunk(LANE), jax.ShapeDtypeStruct((n // KCH, LANE, KCH), bf)),
        (chunk(LANE), jax.ShapeDtypeStruct((n // KCH, LANE, KCH), bf)),
        (colt(G_ROWS), jax.ShapeDtypeStruct((G_ROWS, n), jnp.float32)),
        (colt(512), jax.ShapeDtypeStruct((512, n), bf)),
        (chunk(512), jax.ShapeDtypeStruct((n // KCH, 512, KCH), bf)),
    ]
    return pl.pallas_call(
        _inproj_kernel,
        grid=(n // TM,),
        in_specs=[row(d), _const_spec((1, d)), _const_spec(wp.shape), _const_spec(wt.shape),
                  row(LANE), row(LANE), colt(HALF), colt(HALF),
                  _const_spec(sgu_g.shape), _const_spec(sgu_w.shape), _const_spec(sgu_b.shape)],
        out_specs=[o[0] for o in outs],
        out_shape=[o[1] for o in outs],
        compiler_params=_params("parallel"),
        name="inproj",
    )(x2, g, wp, wt, cos, sin, cost, sint, sgu_g, sgu_w, sgu_b)


def _chunked_tokens(x_ref, g, nc):
    lanes = slice(g * HEAD_DIM, (g + 1) * HEAD_DIM)
    return jnp.concatenate([x_ref[pl.ds(t, nc, stride=CMP_STRIDE), :][:, lanes] for t in range(CMP_STRIDE)],
                           axis=1)


def _compress_k_kernel(x_ref, pa_ref, pb_ref, w1a_ref, w1b_ref, w2_ref, o_ref):
    nc = o_ref.shape[0]
    acc = jnp.zeros(o_ref.shape, jnp.float32)
    for g in range(NSA_GROUPS):
        xg = _chunked_tokens(x_ref, g, nc)
        a = _dot((xg + pa_ref[...]).astype(MXU_DTYPE), w1a_ref[...])
        b = _dot((xg + pb_ref[...]).astype(MXU_DTYPE), w1b_ref[...])
        hid = jax.nn.gelu(a + pltpu.roll(b, nc - 1, 0))
        acc = acc + _dot(hid.astype(MXU_DTYPE), w2_ref[g])
    o_ref[...] = acc.astype(o_ref.dtype)


def _compress_v_kernel(x_ref, pa_ref, pb_ref, w1at_ref, w1bt_ref, w2t_ref, o_ref):
    nc = o_ref.shape[1]
    acc = jnp.zeros(o_ref.shape, jnp.float32)
    for g in range(NSA_GROUPS):
        xg = _chunked_tokens(x_ref, g, nc)
        a = _dot_nt(w1at_ref[...], (xg + pa_ref[...]).astype(MXU_DTYPE))
        b = _dot_nt(w1bt_ref[...], (xg + pb_ref[...]).astype(MXU_DTYPE))
        hid = jax.nn.gelu(a + pltpu.roll(b, nc - 1, 1))
        acc = acc + _dot(w2t_ref[g], hid.astype(MXU_DTYPE))
    o_ref[...] = acc.astype(o_ref.dtype)


def _compress(kvc, b, pos, w1, w2, transposed):
    seq = kvc.shape[0] // b
    nc, cw = seq // CMP_STRIDE, CMP_STRIDE * HEAD_DIM
    pos_flat = pos.reshape(1, CMP_BLOCK * HEAD_DIM)
    pa, pb = pos_flat[:, :cw], pos_flat[:, cw:]
    w1a, w1b = w1[:cw].astype(MXU_DTYPE), w1[cw:].astype(MXU_DTYPE)
    w2p = jnp.zeros((NSA_GROUPS, CMP_HIDDEN, LANE), jnp.float32)
    for g in range(NSA_GROUPS):
        w2p = w2p.at[g, :, g * HEAD_DIM:(g + 1) * HEAD_DIM].set(w2)
    w2p = w2p.astype(MXU_DTYPE)
    x_spec = pl.BlockSpec((seq, kvc.shape[1]), lambda i: (i, 0))
    if not transposed:
        return pl.pallas_call(
            _compress_k_kernel, grid=(b,),
            in_specs=[x_spec, _const_spec(pa.shape), _const_spec(pb.shape), _const_spec(w1a.shape),
                      _const_spec(w1b.shape), _const_spec(w2p.shape)],
            out_specs=pl.BlockSpec((None, nc, LANE), lambda i: (i, 0, 0)),
            out_shape=jax.ShapeDtypeStruct((b, nc, LANE), MXU_DTYPE),
            compiler_params=_params("parallel"), name="compress_k",
        )(kvc, pa, pb, w1a, w1b, w2p)
    w1at, w1bt, w2pt = w1a.T, w1b.T, jnp.swapaxes(w2p, 1, 2)
    return pl.pallas_call(
        _compress_v_kernel, grid=(b,),
        in_specs=[x_spec, _const_spec(pa.shape), _const_spec(pb.shape), _const_spec(w1at.shape),
                  _const_spec(w1bt.shape), _const_spec(w2pt.shape)],
        out_specs=pl.BlockSpec((None, LANE, nc), lambda i: (i, 0, 0)),
        out_shape=jax.ShapeDtypeStruct((b, LANE, nc), MXU_DTYPE),
        compiler_params=_params("parallel"), name="compress_v",
    )(kvc, pa, pb, w1at, w1bt, w2pt)


def _group_queries_t(qt):
    z = jnp.zeros((HEAD_DIM, qt.shape[1]), qt.dtype)
    cols = []
    for h in range(NSA_HEADS):
        slab = qt[h * HEAD_DIM:(h + 1) * HEAD_DIM]
        cols.append(jnp.concatenate([slab, z] if h < HPG else [z, slab], axis=0))
    return jnp.concatenate(cols, axis=1)


def _with_ones_rows(vt):
    return jnp.concatenate([vt, jnp.ones((ONES_ROWS, vt.shape[1]), vt.dtype)], axis=0)


def _normalized(acc):
    dims = acc.shape[0] - ONES_ROWS
    return acc[:dims] * (1.0 / acc[dims:dims + 1])


def _pipelined_chains(n_chain, qk, softmax, pv):
    outs = []
    s_next = qk(0)
    pending = None
    for c in range(n_chain):
        s = s_next
        if c + 1 < n_chain:
            s_next = qk(c + 1)
        r = softmax(c, s)
        if pending is not None:
            outs.append(pv(*pending))
        pending = (c, r)
    outs.append(pv(*pending))
    return outs


def _nsa_kernel(qat_ref, qart_ref, gt_ref, kc_ref, vct_ref, kslc_ref, vslct_ref, kwin_ref, vwint_ref,
                ovt_ref, e_ref, o_ref, m_ref, acc_ref, sa_ref, sb_ref, ocmp_ref, bias_ref):
    qi = pl.program_id(1)
    tq, tk = NSA_TQ, NSA_TK
    nc = kc_ref.shape[0]
    n_chain = NSA_HEADS * tq // CHAIN
    hpc = CHAIN // tq
    t0 = qi * tq
    tok = t0 + lax.broadcasted_iota(jnp.int32, (1, tq), 1)
    tok_c = jnp.concatenate([tok] * hpc, axis=1)
    qc = _group_queries_t(qat_ref[...])
    qr = _group_queries_t(qart_ref[...])
    chain_cols = [slice(c * CHAIN, (c + 1) * CHAIN) for c in range(n_chain)]
    chain_group = [(c * hpc) // HPG for c in range(n_chain)]

    seq = kslc_ref.shape[0]
    tok2 = jnp.concatenate([tok] * NSA_GROUPS, axis=1)
    has_key = tok_c >= CMP_BLOCK - 1
    variant = (CAUSAL_VARIANTS * (t0 + tq) + seq - 1) // seq - 1

    def cmp_topk(rows_c, rows_b, forced_distinct):
        cmp_end = lax.broadcasted_iota(jnp.int32, (rows_c, 1), 0) * CMP_STRIDE + (CMP_BLOCK - 1)
        mask_c = cmp_end <= tok_c
        vo = jnp.concatenate([_with_ones_rows(vct_ref[:, :rows_c]), ovt_ref[:rows_b, :rows_c]], axis=0)
        imp_g = [None] * NSA_GROUPS

        def cmp_softmax(c, s):
            s = jnp.where(mask_c, s, NEG)
            return jnp.exp2(s - jnp.max(s, axis=0, keepdims=True)).astype(MXU_DTYPE)

        def cmp_pv(c, p):
            r = _dot(vo, p)
            inv = jnp.where(has_key, 1.0 / r[LANE:LANE + 1], 0.0)
            imp_c = r[LANE + ONES_ROWS:] * inv
            for i in range(hpc):
                g = chain_group[c]
                part = imp_c[:, i * tq:(i + 1) * tq]
                imp_g[g] = part if imp_g[g] is None else imp_g[g] + part
            return r[:LANE] * inv

        ocmp_ref[...] = jnp.concatenate(_pipelined_chains(
            n_chain, lambda c: _dot(kc_ref[:rows_c, :], qc[:, chain_cols[c]]), cmp_softmax, cmp_pv), axis=1)
        imp = jnp.concatenate(imp_g, axis=1)

        blk = lax.broadcasted_iota(jnp.int32, (rows_b, 1), 0)
        blk_f = blk.astype(jnp.float32)
        cur = tok2 // SLC_BLOCK
        forced = (blk == 0) | (blk == cur) | (blk == cur - 1)
        score = jnp.where(blk * SLC_BLOCK <= tok2, imp, -BIG)
        if forced_distinct:
            rounds = SLC_TOPK - 3
            bias = jnp.where(forced, 0.0, NEG)
            score = jnp.where(forced, REMOVED, score)
        else:
            rounds = SLC_TOPK
            bias = jnp.full(score.shape, NEG, jnp.float32)
            score = jnp.where(forced, BIG, score)
        for _ in range(rounds):
            best = jnp.max(score, axis=0, keepdims=True)
            first = jnp.min(jnp.where(score == best, blk_f, float(NSP)), axis=0, keepdims=True)
            pick = blk_f == first
            bias = jnp.where(pick, 0.0, bias)
            score = jnp.where(pick, REMOVED, score)
        bias_ref[:rows_b, :] = bias.astype(MXU_DTYPE)
        if rows_b < NSP:
            bias_ref[rows_b:, :] = jnp.full((NSP - rows_b, NSA_GROUPS * tq), NEG, MXU_DTYPE)

    for v in range(CAUSAL_VARIANTS):
        rows_c = nc * (v + 1) // CAUSAL_VARIANTS
        rows_b = min(NSP, -(-(seq // SLC_BLOCK * (v + 1) // CAUSAL_VARIANTS) // 32) * 32)
        min_t0 = seq * v // CAUSAL_VARIANTS - tq + 1
        pl.when(variant == v)(functools.partial(cmp_topk, rows_c, rows_b, v > 0 and min_t0 >= 2 * SLC_BLOCK))
    o_cmp = ocmp_ref[...]
    bias = bias_ref[...]

    qa = []
    for c in range(n_chain):
        g = chain_group[c]
        bias_c = jnp.concatenate([bias[:, g * tq:(g + 1) * tq]] * hpc, axis=1)
        qa.append(jnp.concatenate([qr[:, chain_cols[c]], bias_c], axis=0))
    m_ref[...] = jnp.full(m_ref.shape, NEG, jnp.float32)
    acc_ref[...] = jnp.zeros(acc_ref.shape, jnp.float32)
    cpt = tk // KCH

    def slc_qk(j, c):
        k0 = pl.multiple_of(j * tk, tk)
        ka = jnp.concatenate([kslc_ref[pl.ds(k0, tk), :], e_ref[pl.ds(k0, tk), :]], axis=1)
        return _dot(ka, qa[c])

    def slc_tile(j, cur_ref, nxt_ref, masked):
        vt = _with_ones_rows(jnp.concatenate([vslct_ref[j * cpt + c] for c in range(cpt)], axis=1))
        kpos = j * tk + lax.broadcasted_iota(jnp.int32, (tk, 1), 0)
        m_all, acc_all = m_ref[...], acc_ref[...]
        m_out, acc_out = [], []
        pending = None
        for c in range(n_chain):
            cols = chain_cols[c]
            if nxt_ref is not None:
                nxt_ref[c] = slc_qk(jnp.maximum(j - 1, 0), c)
            s = cur_ref[c]
            if masked:
                s = jnp.where(kpos <= tok_c, s, NEG)
            m_old = m_all[:, cols]
            m_new = jnp.maximum(m_old, jnp.max(s, axis=0, keepdims=True))
            alpha = jnp.exp2(m_old - m_new)
            p = jnp.exp2(s - m_new)
            m_out.append(m_new)
            if pending is not None:
                acc_out.append(pending[0] + _dot(vt, pending[1]))
            pending = (acc_all[:, cols] * alpha, p.astype(MXU_DTYPE))
        acc_out.append(pending[0] + _dot(vt, pending[1]))
        m_ref[...] = jnp.concatenate(m_out, axis=1)
        acc_ref[...] = jnp.concatenate(acc_out, axis=1)

    last = t0 // tk
    for c in range(n_chain):
        sa_ref[c] = slc_qk(last, c)
    slc_tile(last, sa_ref, sb_ref, True)

    def pair(i, carry):
        j = last - 1 - 2 * i
        slc_tile(j, sb_ref, sa_ref, False)
        slc_tile(j - 1, sa_ref, sb_ref, False)
        return carry

    lax.fori_loop(0, last // 2, pair, 0)

    @pl.when(last % 2 == 1)
    def _():
        slc_tile(0, sb_ref, None, False)

    o_slc = _normalized(acc_ref[...])

    nwc = (WINDOW + tq) // KCH
    c0 = jnp.maximum(qi * (tq // KCH) - WINDOW // KCH, 0)
    w0 = pl.multiple_of(c0 * KCH, KCH)
    kpos = w0 + lax.broadcasted_iota(jnp.int32, (WINDOW + tq, 1), 0)
    valid_w = (kpos <= tok_c) & (kpos > tok_c - WINDOW)
    vt_w = _with_ones_rows(jnp.concatenate([vwint_ref[c0 + c] for c in range(nwc)], axis=1))

    def win_softmax(c, s):
        s = jnp.where(valid_w, s, NEG)
        return jnp.exp2(s - jnp.max(s, axis=0, keepdims=True)).astype(MXU_DTYPE)

    o_win = jnp.concatenate(_pipelined_chains(
        n_chain, lambda c: _dot(kwin_ref[pl.ds(w0, WINDOW + tq), :], qr[:, chain_cols[c]]), win_softmax,
        lambda c, p: _normalized(_dot(vt_w, p))), axis=1)

    gt = gt_ref[...]
    outs = []
    for h in range(NSA_HEADS):
        rows = slice((h // HPG) * HEAD_DIM, (h // HPG + 1) * HEAD_DIM)
        cols = slice(h * tq, (h + 1) * tq)
        outs.append(gt[3 * h:3 * h + 1] * o_cmp[rows, cols] + gt[3 * h + 1:3 * h + 2] * o_slc[rows, cols]
                    + gt[3 * h + 2:3 * h + 3] * o_win[rows, cols])
    o_ref[...] = jnp.concatenate(outs, axis=0).T.astype(o_ref.dtype)


def _nsa(bsz, seq, qat, qart, gt, kc, vct, kslc, vslct, kwin, vwint, ovt, onehot):
    n = bsz * seq
    nq = seq // NSA_TQ
    nc = kc.shape[1]
    n_chain = NSA_HEADS * NSA_TQ // CHAIN
    qspec = lambda r: pl.BlockSpec((r, NSA_TQ), lambda b, i: (0, b * nq + i))
    kspec = pl.BlockSpec((seq, LANE), lambda b, i: (b, 0))
    vspec = pl.BlockSpec((seq // KCH, LANE, KCH), lambda b, i: (b, 0, 0))
    return pl.pallas_call(
        _nsa_kernel,
        grid=(bsz, nq),
        in_specs=[qspec(512), qspec(512), qspec(G_ROWS),
                  pl.BlockSpec((None, nc, LANE), lambda b, i: (b, 0, 0)),
                  pl.BlockSpec((None, LANE, nc), lambda b, i: (b, 0, 0)),
                  kspec, vspec, kspec, vspec, _const_spec(ovt.shape), _const_spec(onehot.shape)],
        out_specs=pl.BlockSpec((NSA_TQ, 512), lambda b, i: (b * nq + i, 0)),
        out_shape=jax.ShapeDtypeStruct((n, 512), MXU_DTYPE),
        scratch_shapes=[pltpu.VMEM((1, NSA_HEADS * NSA_TQ), jnp.float32),
                        pltpu.VMEM((LANE + ONES_ROWS, NSA_HEADS * NSA_TQ), jnp.float32),
                        pltpu.VMEM((n_chain, NSA_TK, CHAIN), jnp.float32),
                        pltpu.VMEM((n_chain, NSA_TK, CHAIN), jnp.float32),
                        pltpu.VMEM((LANE, NSA_HEADS * NSA_TQ), jnp.float32),
                        pltpu.VMEM((NSP, NSA_GROUPS * NSA_TQ), MXU_DTYPE)],
        compiler_params=_params("parallel", "parallel"),
        name="nsa_attention",
    )(qat, qart, gt, kc, vct, kslc, vslct, kwin, vwint, ovt, onehot)


def _diff_kernel(lambda_init, q_ref, k_ref, vt_ref, lq1_ref, lk1_ref, lq2_ref, lk2_ref, subln_ref,
                 o_ref, m_ref, acc_ref, sa_ref, sb_ref):
    qi = pl.program_id(2)
    tq, tk = DIFF_TQ, DIFF_TK
    cpt = tk // KCH
    t0 = qi * tq
    qt = q_ref[...]
    z = jnp.zeros((HEAD_DIM, tq), qt.dtype)
    qp = jnp.concatenate([jnp.concatenate([qt[:HEAD_DIM], z], axis=0),
                          jnp.concatenate([z, qt[HEAD_DIM:]], axis=0)], axis=1)
    tok = t0 + lax.broadcasted_iota(jnp.int32, (1, CHAIN), 1)

    m_ref[...] = jnp.full(m_ref.shape, NEG, jnp.float32)
    acc_ref[...] = jnp.zeros(acc_ref.shape, jnp.float32)

    n_chain = 2 * tq // CHAIN

    def qk(j, c):
        k = k_ref[pl.ds(pl.multiple_of(j * tk, tk), tk), :]
        return _dot(k, qp[:, c * CHAIN:(c + 1) * CHAIN])

    def tile(j, cur_ref, nxt_ref, masked):
        vt = _with_ones_rows(jnp.concatenate([vt_ref[j * cpt + c] for c in range(cpt)], axis=1))
        kpos = j * tk + lax.broadcasted_iota(jnp.int32, (tk, 1), 0)
        m_all, acc_all = m_ref[...], acc_ref[...]
        m_out, acc_out = [], []
        pending = None
        for c in range(n_chain):
            cols = slice(c * CHAIN, (c + 1) * CHAIN)
            if nxt_ref is not None:
                nxt_ref[c] = qk(jnp.maximum(j - 1, 0), c)
            s = cur_ref[c]
            if masked:
                s = jnp.where(kpos <= tok + (c * CHAIN) % tq, s, NEG)
            m_old = m_all[:, cols]
            m_new = jnp.maximum(m_old, jnp.max(s, axis=0, keepdims=True))
            alpha = jnp.exp2(m_old - m_new)
            p = jnp.exp2(s - m_new)
            m_out.append(m_new)
            if pending is not None:
                acc_out.append(pending[0] + _dot(vt, pending[1]))
            pending = (acc_all[:, cols] * alpha, p.astype(MXU_DTYPE))
        acc_out.append(pending[0] + _dot(vt, pending[1]))
        m_ref[...] = jnp.concatenate(m_out, axis=1)
        acc_ref[...] = jnp.concatenate(acc_out, axis=1)

    last = t0 // tk
    for c in range(n_chain):
        sa_ref[c] = qk(last, c)
    tile(last, sa_ref, sb_ref, True)

    def pair(i, carry):
        j = last - 1 - 2 * i
        tile(j, sb_ref, sa_ref, False)
        tile(j - 1, sa_ref, sb_ref, False)
        return carry

    lax.fori_loop(0, last // 2, pair, 0)

    @pl.when(last % 2 == 1)
    def _():
        tile(0, sb_ref, None, False)

    lam = (jnp.exp(jnp.sum(lq1_ref[...] * lk1_ref[...], axis=1, keepdims=True))
           - jnp.exp(jnp.sum(lq2_ref[...] * lk2_ref[...], axis=1, keepdims=True)) + lambda_init)
    o = _normalized(acc_ref[...])
    o = o[:, :tq] - lam * o[:, tq:]
    o = o * lax.rsqrt(jnp.mean(o * o, axis=0, keepdims=True) + NORM_EPS)
    o = o * subln_ref[...] * (1.0 - lambda_init)
    o_ref[...] = o.T.astype(o_ref.dtype)


def _diff(bsz, seq, lambda_init, qbt, kb, vbt, lq1, lk1, lq2, lk2, subln):
    n = bsz * seq
    nq = seq // DIFF_TQ
    vec = _const_spec((1, HEAD_DIM))
    return pl.pallas_call(
        functools.partial(_diff_kernel, lambda_init),
        grid=(bsz, DIFF_HEADS, nq),
        in_specs=[pl.BlockSpec((DIFF_V_DIM, DIFF_TQ), lambda b, h, i: (h, b * nq + i)),
                  pl.BlockSpec((seq, LANE), lambda b, h, i: (b, h)),
                  pl.BlockSpec((seq // KCH, DIFF_V_DIM, KCH), lambda b, h, i: (b, h, 0)),
                  vec, vec, vec, vec, _const_spec((DIFF_V_DIM, 1))],
        out_specs=pl.BlockSpec((DIFF_TQ, DIFF_V_DIM), lambda b, h, i: (b * nq + i, h)),
        out_shape=jax.ShapeDtypeStruct((n, DIFF_HEADS * DIFF_V_DIM), MXU_DTYPE),
        scratch_shapes=[pltpu.VMEM((1, 2 * DIFF_TQ), jnp.float32),
                        pltpu.VMEM((DIFF_V_DIM + ONES_ROWS, 2 * DIFF_TQ), jnp.float32),
                        pltpu.VMEM((2 * DIFF_TQ // CHAIN, DIFF_TK, CHAIN), jnp.float32),
                        pltpu.VMEM((2 * DIFF_TQ // CHAIN, DIFF_TK, CHAIN), jnp.float32)],
        compiler_params=_params("parallel", "parallel", "parallel"),
        name="diff_attention",
    )(qbt, kb, vbt, lq1, lk1, lq2, lk2, subln)


def _merge_kernel(x_ref, g_ref, wm_ref, bm_ref, oa_ref, ob_ref, oc_ref, wa_ref, wb_ref, wc_ref, wo_ref, o_ref):
    x = x_ref[...]
    d = x.shape[1]
    hb = _rmsnorm_rows(x, g_ref[...]).astype(MXU_DTYPE)
    mixed = None
    for i, (br_ref, w_ref) in enumerate(((oa_ref, wa_ref), (ob_ref, wb_ref), (oc_ref, wc_ref))):
        cols = slice(i * d, (i + 1) * d)
        gate = jax.nn.sigmoid(_dot(hb, wm_ref[:, cols]) + bm_ref[:, cols])
        term = gate * _dot(br_ref[...], w_ref[...])
        mixed = term if mixed is None else mixed + term
    o_ref[...] = x + _dot(mixed.astype(MXU_DTYPE), wo_ref[...])


def _merge(x2, g, wm, bm, oa, ob, oc, wa, wb, wc, wo):
    n, d = x2.shape
    row = lambda w: pl.BlockSpec((TM, w), lambda i: (i, 0))
    return pl.pallas_call(
        _merge_kernel,
        grid=(n // TM,),
        in_specs=[row(d), _const_spec(g.shape), _const_spec(wm.shape), _const_spec(bm.shape),
                  row(BRANCH), row(BRANCH), row(BRANCH),
                  _const_spec(wa.shape), _const_spec(wb.shape), _const_spec(wc.shape), _const_spec(wo.shape)],
        out_specs=row(d),
        out_shape=jax.ShapeDtypeStruct((n, d), jnp.float32),
        compiler_params=_params("parallel"),
        name="merge_out",
    )(x2, g, wm, bm, oa, ob, oc, wa, wb, wc, wo)


def _ffn_kernel(final, n_chunks, x_ref, g_ref, w1_ref, w3_ref, w2_ref, gf_ref, o_ref):
    x = x_ref[...]
    hb = _rmsnorm_rows(x, g_ref[...]).astype(MXU_DTYPE)
    cw = w1_ref.shape[1] // n_chunks
    y = x
    for c in range(n_chunks):
        cols = slice(c * cw, (c + 1) * cw)
        act = jax.nn.silu(_dot(hb, w1_ref[:, cols])) * _dot(hb, w3_ref[:, cols])
        y = y + _dot(act.astype(MXU_DTYPE), w2_ref[cols, :])
    if final:
        y = _rmsnorm_rows(y, gf_ref[...])
    o_ref[...] = y


def _ffn(x2, g, w1, w3, w2, gf, final):
    n, d = x2.shape
    dff = w1.shape[1]
    n_chunks = 2 if dff % (2 * LANE) == 0 else 1
    row = pl.BlockSpec((TM, d), lambda i: (i, 0))
    return pl.pallas_call(
        functools.partial(_ffn_kernel, final, n_chunks),
        grid=(n // TM,),
        in_specs=[row, _const_spec(g.shape), _const_spec(w1.shape), _const_spec(w3.shape),
                  _const_spec(w2.shape), _const_spec(gf.shape)],
        out_specs=row,
        out_shape=jax.ShapeDtypeStruct((n, d), jnp.float32),
        compiler_params=_params("parallel"),
        name="ffn",
    )(x2, g, w1, w3, w2, gf)


def _overlap_t(seq):
    nc = seq // CMP_STRIDE
    c_start = jnp.arange(nc)[None, :] * CMP_STRIDE
    s_start = jnp.arange(NSP)[:, None] * SLC_BLOCK
    ov = (c_start < s_start + SLC_BLOCK) & (c_start + CMP_BLOCK > s_start)
    return ov.astype(MXU_DTYPE)


def _block_onehot(seq):
    return (jnp.arange(seq)[:, None] // SLC_BLOCK == jnp.arange(NSP)[None, :]).astype(MXU_DTYPE)


def kernel(x, positions, attn_norm, w_in, cmp_pos_k, cmp_k_w1, cmp_k_w2, cmp_pos_v, cmp_v_w1, cmp_v_w2, diff_lq1, diff_lk1, diff_lq2, diff_lk2, diff_subln, sgu_norm, sgu_w, sgu_b, w_branch_a, w_branch_b, w_branch_c, w_merge, b_merge, w_out, ffn_norm, w_ffn1, w_ffn3, w_ffn2, final_norm):
    bsz, seq, d = x.shape
    depth = w_in.shape[0]
    n = bsz * seq
    assert seq % NSA_TK == 0 and seq >= WINDOW + NSA_TQ and seq // SLC_BLOCK <= NSP and n % TM == 0
    bf = MXU_DTYPE

    tables = _rope_tables(positions)
    ovt = _overlap_t(seq)
    onehot = _block_onehot(seq)
    causal = jnp.tril(jnp.ones((SGU_CHUNK, SGU_CHUNK), dtype=bool))
    x2 = x.reshape(n, d)
    row = lambda v: v.reshape(1, -1)

    for l in range(depth):
        lambda_init = 0.8 - 0.6 * math.exp(-0.3 * l)
        w = w_in[l]
        wp = jnp.concatenate([w[:, 512:768], w[:, 768:896], w[:, 1024:1152], w[:, 1816:2328], w[:, 2840:3864]],
                             axis=1).astype(bf)
        wt = jnp.concatenate([w[:, 0:512], w[:, 896:1024], w[:, 1152:1280], w[:, 1280:1304],
                              jnp.zeros((d, G_ROWS - 3 * NSA_HEADS), w.dtype), w[:, 1304:1816], w[:, 2328:2840]],
                             axis=1).T.astype(bf)
        sgu_wm = jnp.where(causal[None], sgu_w[l], 0.0).astype(bf)
        sgu_bias = jnp.repeat(sgu_b[l].T, SGU_WIDTH // SGU_GROUPS, axis=1)

        (kcmp, vcmp, kslc, kwin, kb, oc, qat, qart, vslct, vwint, gt, qbt, vbt) = _inproj(
            x2, row(attn_norm[l]), wp, wt, tables, row(sgu_norm[l]), sgu_wm, sgu_bias)

        kc = _compress(kcmp, bsz, cmp_pos_k[l], cmp_k_w1[l], cmp_k_w2[l], transposed=False)
        vct = _compress(vcmp, bsz, cmp_pos_v[l], cmp_v_w1[l], cmp_v_w2[l], transposed=True)

        oa = _nsa(bsz, seq, qat, qart, gt, kc, vct, kslc, vslct, kwin, vwint, ovt, onehot)
        ob = _diff(bsz, seq, lambda_init, qbt, kb, vbt, row(diff_lq1[l]), row(diff_lk1[l]),
                   row(diff_lq2[l]), row(diff_lk2[l]), diff_subln[l].reshape(DIFF_V_DIM, 1))

        x2 = _merge(x2, row(attn_norm[l]), w_merge[l].astype(bf), row(b_merge[l]), oa, ob, oc,
                    w_branch_a[l].astype(bf), w_branch_b[l].astype(bf), w_branch_c[l].astype(bf),
                    w_out[l].astype(bf))
        x2 = _ffn(x2, row(ffn_norm[l]), w_ffn1[l].astype(bf), w_ffn3[l].astype(bf), w_ffn2[l].astype(bf),
                  row(final_norm), final=(l == depth - 1))
    return x2.reshape(bsz, seq, d)
```

```python
import functools
import math

import jax
import jax.numpy as jnp
from jax import lax
from jax.experimental import pallas as pl
from jax.experimental.pallas import tpu as pltpu

HEAD_DIM = 64
HALF = HEAD_DIM // 2
ROPE_THETA = 10000.0
NORM_EPS = 1e-6
BIG = 1e9
NEG = -1e30
REMOVED = -3e38

NSA_HEADS = 8
NSA_GROUPS = 2
HPG = NSA_HEADS // NSA_GROUPS
CMP_BLOCK = 32
CMP_STRIDE = 16
CMP_HIDDEN = 256
SLC_BLOCK = 64
SLC_TOPK = 16
WINDOW = 512
DIFF_HEADS = 4
DIFF_V_DIM = 2 * HEAD_DIM
SGU_CHUNK = 128
SGU_GROUPS = 4
SGU_WIDTH = 512
BRANCH = 512
SCALE = HEAD_DIM ** -0.5
QSCALE = SCALE * math.log2(math.e)
ONES_ROWS = 16

LANE = 128
NSP = 128
MXU_DTYPE = jnp.bfloat16
VMEM_LIMIT = 56 * 1024 * 1024

TM = 512
NSA_TQ = 256
NSA_TK = 512
DIFF_TQ = 512
DIFF_TK = 512
CHAIN = 256
SWEEP_UNROLL = 4
CAUSAL_VARIANTS = 4
KCH = 128


def _params(*sem):
    return pltpu.CompilerParams(dimension_semantics=sem, vmem_limit_bytes=VMEM_LIMIT)


def _const_spec(shape):
    nd = len(shape)
    return pl.BlockSpec(shape, lambda *_: (0,) * nd, pipeline_mode=pl.Buffered(1))


def _dot(a, b):
    return jnp.dot(a, b, preferred_element_type=jnp.float32)


def _dot_nt(a, b):
    return lax.dot_general(a, b, (((1,), (1,)), ((), ())), preferred_element_type=jnp.float32)


def _rmsnorm_rows(x, g):
    return x * lax.rsqrt(jnp.mean(x * x, axis=-1, keepdims=True) + NORM_EPS) * g


def _tables_kernel(pos_row_ref, f_col_ref, cos_ref, sin_ref, cost_ref, sint_ref):
    angt = f_col_ref[...] * pos_row_ref[...].astype(jnp.float32)
    cost, sint = jnp.cos(angt), jnp.sin(angt)
    cost_ref[...] = cost
    sint_ref[...] = sint
    reps = LANE // HALF
    cos_ref[...] = jnp.concatenate([cost] * reps, axis=0).T
    sin = jnp.concatenate([sint] * reps, axis=0).T
    lane = lax.broadcasted_iota(jnp.int32, sin.shape, 1)
    sin_ref[...] = jnp.where((lane % HEAD_DIM) < HALF, -sin, sin)


def _rope_tables(positions):
    n = positions.size
    inv_freq = ROPE_THETA ** (-jnp.arange(HALF, dtype=jnp.float32) / HALF)
    f_col = inv_freq.reshape(HALF, 1)
    pos_row = positions.reshape(1, n)
    return pl.pallas_call(
        _tables_kernel,
        grid=(n // TM,),
        in_specs=[pl.BlockSpec((1, TM), lambda i: (0, i)), _const_spec((HALF, 1))],
        out_specs=[pl.BlockSpec((TM, LANE), lambda i: (i, 0)),
                   pl.BlockSpec((TM, LANE), lambda i: (i, 0)),
                   pl.BlockSpec((HALF, TM), lambda i: (0, i)),
                   pl.BlockSpec((HALF, TM), lambda i: (0, i))],
        out_shape=[jax.ShapeDtypeStruct((n, LANE), jnp.float32),
                   jax.ShapeDtypeStruct((n, LANE), jnp.float32),
                   jax.ShapeDtypeStruct((HALF, n), jnp.float32),
                   jax.ShapeDtypeStruct((HALF, n), jnp.float32)],
        compiler_params=_params("parallel"),
        name="rope_tables",
    )(pos_row, f_col)


P_KVC, P_KSLC, P_KWIN, P_KB, P_UV, P_END = 0, 256, 384, 512, 1024, 2048
T_QA, T_VSLC, T_VWIN, T_G, T_QB, T_VB, T_END = 0, 512, 640, 768, 800, 1312, 1824
G_ROWS = T_QB - T_G


def _rope_plain(y, cos, sin):
    lane = lax.broadcasted_iota(jnp.int32, y.shape, 1)
    swapped = jnp.where((lane % HEAD_DIM) < HALF, pltpu.roll(y, LANE - HALF, 1), pltpu.roll(y, HALF, 1))
    return y * cos + swapped * sin


def _rope_t(y, cos, sin):
    y1, y2 = y[:HALF], y[HALF:]
    return jnp.concatenate([y1 * cos - y2 * sin, y2 * cos + y1 * sin], axis=0)


def _inproj_kernel(x_ref, g_ref, wp_ref, wt_ref, cos_ref, sin_ref, cost_ref, sint_ref,
                   sgu_g_ref, sgu_w_ref, sgu_b_ref,
                   kcmp_ref, vcmp_ref, kslc_ref, kwin_ref, kb_ref, oc_ref,
                   qat_ref, qart_ref, vslct_ref, vwint_ref, gt_ref, qbt_ref, vbt_ref):
    x = x_ref[...]
    hb = _rmsnorm_rows(x, g_ref[...]).astype(MXU_DTYPE)
    cos, sin = cos_ref[...], sin_ref[...]
    cost, sint = cost_ref[...], sint_ref[...]
    tm = x.shape[0]

    z = jax.nn.gelu(_dot(hb, wp_ref[:, P_UV:P_END]))
    u = z[:, :SGU_WIDTH]
    vn = _rmsnorm_rows(z[:, SGU_WIDTH:], sgu_g_ref[...]).astype(MXU_DTYPE)

    kvc = _dot(hb, wp_ref[:, P_KVC:P_KSLC])
    kcmp_ref[...] = kvc[:, :LANE]
    vcmp_ref[...] = kvc[:, LANE:]
    kslc_ref[...] = _rope_plain(_dot(hb, wp_ref[:, P_KSLC:P_KWIN]), cos, sin).astype(kslc_ref.dtype)
    kwin_ref[...] = _rope_plain(_dot(hb, wp_ref[:, P_KWIN:P_KB]), cos, sin).astype(kwin_ref.dtype)
    kb = _dot(hb, wp_ref[:, P_KB:P_UV])
    for c in range((P_UV - P_KB) // LANE):
        sl = slice(c * LANE, (c + 1) * LANE)
        kb_ref[:, sl] = _rope_plain(kb[:, sl], cos, sin).astype(kb_ref.dtype)

    qat = _dot_nt(wt_ref[T_QA:T_VSLC, :], hb) * QSCALE
    qat_ref[...] = qat.astype(qat_ref.dtype)
    for h in range(NSA_HEADS):
        sl = slice(h * HEAD_DIM, (h + 1) * HEAD_DIM)
        qart_ref[sl, :] = _rope_t(qat[sl], cost, sint).astype(qart_ref.dtype)
    vst = _dot_nt(wt_ref[T_VSLC:T_VWIN, :], hb).astype(vslct_ref.dtype)
    vwt = _dot_nt(wt_ref[T_VWIN:T_G, :], hb).astype(vwint_ref.dtype)
    for c in range(tm // KCH):
        sl = slice(c * KCH, (c + 1) * KCH)
        vslct_ref[c] = vst[:, sl]
        vwint_ref[c] = vwt[:, sl]
    gt_ref[...] = jax.nn.sigmoid(_dot_nt(wt_ref[T_G:T_QB, :], hb))
    qbt = _dot_nt(wt_ref[T_QB:T_VB, :], hb) * QSCALE
    for h in range(2 * DIFF_HEADS):
        sl = slice(h * HEAD_DIM, (h + 1) * HEAD_DIM)
        qbt_ref[sl, :] = _rope_t(qbt[sl], cost, sint).astype(qbt_ref.dtype)
    vbt = _dot_nt(wt_ref[T_VB:T_END, :], hb).astype(vbt_ref.dtype)
    for c in range(tm // KCH):
        vbt_ref[c] = vbt[:, c * KCH:(c + 1) * KCH]

    gdim = SGU_WIDTH // SGU_GROUPS
    for ci in range(tm // SGU_CHUNK):
        rows = slice(ci * SGU_CHUNK, (ci + 1) * SGU_CHUNK)
        for gi in range(SGU_GROUPS):
            cols = slice(gi * gdim, (gi + 1) * gdim)
            s = _dot(sgu_w_ref[gi], vn[rows, cols]) + sgu_b_ref[:, cols]
            oc_ref[rows, cols] = (u[rows, cols] * s).astype(oc_ref.dtype)


def _inproj(x2, g, wp, wt, tables, sgu_g, sgu_w, sgu_b):
    n, d = x2.shape
    cos, sin, cost, sint = tables
    row = lambda w: pl.BlockSpec((TM, w), lambda i: (i, 0))
    colt = lambda r: pl.BlockSpec((r, TM), lambda i: (0, i))
    chunk = lambda r: pl.BlockSpec((TM // KCH, r, KCH), lambda i: (i, 0, 0))
    bf = MXU_DTYPE
    outs = [
        (row(LANE), jax.ShapeDtypeStruct((n, LANE), jnp.float32)),
        (row(LANE), jax.ShapeDtypeStruct((n, LANE), jnp.float32)),
        (row(LANE), jax.ShapeDtypeStruct((n, LANE), bf)),
        (row(LANE), jax.ShapeDtypeStruct((n, LANE), bf)),
        (row(512), jax.ShapeDtypeStruct((n, 512), bf)),
        (row(SGU_WIDTH), jax.ShapeDtypeStruct((n, SGU_WIDTH), bf)),
        (colt(512), jax.ShapeDtypeStruct((512, n), bf)),
        (colt(512), jax.ShapeDtypeStruct((512, n), bf)),
        (chunk(LANE), jax.ShapeDtypeStruct((n // KCH, LANE, KCH), bf)),
        (chunk(LANE), jax.ShapeDtypeStruct((n // KCH, LANE, KCH), bf)),
        (colt(G_ROWS), jax.ShapeDtypeStruct((G_ROWS, n), jnp.float32)),
        (colt(512), jax.ShapeDtypeStruct((512, n), bf)),
        (chunk(512), jax.ShapeDtypeStruct((n // KCH, 512, KCH), bf)),
    ]
    return pl.pallas_call(
        _inproj_kernel,
        grid=(n // TM,),
        in_specs=[row(d), _const_spec((1, d)), _const_spec(wp.shape), _const_spec(wt.shape),
                  row(LANE), row(LANE), colt(HALF), colt(HALF),
                  _const_spec(sgu_g.shape), _const_spec(sgu_w.shape), _const_spec(sgu_b.shape)],
        out_specs=[o[0] for o in outs],
        out_shape=[o[1] for o in outs],
        compiler_params=_params("parallel"),
        name="inproj",
    )(x2, g, wp, wt, cos, sin, cost, sint, sgu_g, sgu_w, sgu_b)


def _chunked_tokens(x_ref, g, nc):
    lanes = slice(g * HEAD_DIM, (g + 1) * HEAD_DIM)
    return jnp.concatenate([x_ref[pl.ds(t, nc, stride=CMP_STRIDE), :][:, lanes] for t in range(CMP_STRIDE)],
                           axis=1)


def _compress_k_kernel(x_ref, pa_ref, pb_ref, w1a_ref, w1b_ref, w2_ref, o_ref):
    nc = o_ref.shape[0]
    acc = jnp.zeros(o_ref.shape, jnp.float32)
    for g in range(NSA_GROUPS):
        xg = _chunked_tokens(x_ref, g, nc)
        a = _dot((xg + pa_ref[...]).astype(MXU_DTYPE), w1a_ref[...])
        b = _dot((xg + pb_ref[...]).astype(MXU_DTYPE), w1b_ref[...])
        hid = jax.nn.gelu(a + pltpu.roll(b, nc - 1, 0))
        acc = acc + _dot(hid.astype(MXU_DTYPE), w2_ref[g])
    o_ref[...] = acc.astype(o_ref.dtype)


def _compress_v_kernel(x_ref, pa_ref, pb_ref, w1at_ref, w1bt_ref, w2t_ref, o_ref):
    nc = o_ref.shape[1]
    acc = jnp.zeros(o_ref.shape, jnp.float32)
    for g in range(NSA_GROUPS):
        xg = _chunked_tokens(x_ref, g, nc)
        a = _dot_nt(w1at_ref[...], (xg + pa_ref[...]).astype(MXU_DTYPE))
        b = _dot_nt(w1bt_ref[...], (xg + pb_ref[...]).astype(MXU_DTYPE))
        hid = jax.nn.gelu(a + pltpu.roll(b, nc - 1, 1))
        acc = acc + _dot(w2t_ref[g], hid.astype(MXU_DTYPE))
    o_ref[...] = acc.astype(o_ref.dtype)


def _compress(kvc, b, pos, w1, w2, transposed):
    seq = kvc.shape[0] // b
    nc, cw = seq // CMP_STRIDE, CMP_STRIDE * HEAD_DIM
    pos_flat = pos.reshape(1, CMP_BLOCK * HEAD_DIM)
    pa, pb = pos_flat[:, :cw], pos_flat[:, cw:]
    w1a, w1b = w1[:cw].astype(MXU_DTYPE), w1[cw:].astype(MXU_DTYPE)
    w2p = jnp.zeros((NSA_GROUPS, CMP_HIDDEN, LANE), jnp.float32)
    for g in range(NSA_GROUPS):
        w2p = w2p.at[g, :, g * HEAD_DIM:(g + 1) * HEAD_DIM].set(w2)
    w2p = w2p.astype(MXU_DTYPE)
    x_spec = pl.BlockSpec((seq, kvc.shape[1]), lambda i: (i, 0))
    if not transposed:
        return pl.pallas_call(
            _compress_k_kernel, grid=(b,),
            in_specs=[x_spec, _const_spec(pa.shape), _const_spec(pb.shape), _const_spec(w1a.shape),
                      _const_spec(w1b.shape), _const_spec(w2p.shape)],
            out_specs=pl.BlockSpec((None, nc, LANE), lambda i: (i, 0, 0)),
            out_shape=jax.ShapeDtypeStruct((b, nc, LANE), MXU_DTYPE),
            compiler_params=_params("parallel"), name="compress_k",
        )(kvc, pa, pb, w1a, w1b, w2p)
    w1at, w1bt, w2pt = w1a.T, w1b.T, jnp.swapaxes(w2p, 1, 2)
    return pl.pallas_call(
        _compress_v_kernel, grid=(b,),
        in_specs=[x_spec, _const_spec(pa.shape), _const_spec(pb.shape), _const_spec(w1at.shape),
                  _const_spec(w1bt.shape), _const_spec(w2pt.shape)],
        out_specs=pl.BlockSpec((None, LANE, nc), lambda i: (i, 0, 0)),
        out_shape=jax.ShapeDtypeStruct((b, LANE, nc), MXU_DTYPE),
        compiler_params=_params("parallel"), name="compress_v",
    )(kvc, pa, pb, w1at, w1bt, w2pt)


def _group_queries_t(qt):
    z = jnp.zeros((HEAD_DIM, qt.shape[1]), qt.dtype)
    cols = []
    for h in range(NSA_HEADS):
        slab = qt[h * HEAD_DIM:(h + 1) * HEAD_DIM]
        cols.append(jnp.concatenate([slab, z] if h < HPG else [z, slab], axis=0))
    return jnp.concatenate(cols, axis=1)


def _with_ones_rows(vt):
    return jnp.concatenate([vt, jnp.ones((ONES_ROWS, vt.shape[1]), vt.dtype)], axis=0)


def _normalized(acc):
    dims = acc.shape[0] - ONES_ROWS
    return acc[:dims] * (1.0 / acc[dims:dims + 1])


def _visit_tiles(last, tile, sa_ref, sb_ref):
    def slots(u):
        return (sb_ref, sa_ref) if u % 2 == 0 else (sa_ref, sb_ref)

    def group(i, carry):
        j = last - 1 - SWEEP_UNROLL * i
        for u in range(SWEEP_UNROLL):
            tile(j - u, *slots(u), False)
        return carry

    lax.fori_loop(0, last // SWEEP_UNROLL, group, 0)
    rem = last % SWEEP_UNROLL
    for u in range(SWEEP_UNROLL - 1):
        cur_ref, nxt_ref = slots(u)
        if u == SWEEP_UNROLL - 2:
            nxt_ref = None
        pl.when(rem > u)(functools.partial(tile, rem - 1 - u, cur_ref, nxt_ref, False))


def _pipelined_chains(n_chain, qk, softmax, pv):
    outs = []
    s_next = qk(0)
    pending = None
    for c in range(n_chain):
        s = s_next
        if c + 1 < n_chain:
            s_next = qk(c + 1)
        r = softmax(c, s)
        if pending is not None:
            outs.append(pv(*pending))
        pending = (c, r)
    outs.append(pv(*pending))
    return outs


def _nsa_kernel(qat_ref, qart_ref, gt_ref, kc_ref, vct_ref, kslc_ref, vslct_ref, kwin_ref, vwint_ref,
                ovt_ref, e_ref, o_ref, m_ref, acc_ref, sa_ref, sb_ref, ocmp_ref, bias_ref):
    qi = pl.program_id(1)
    tq, tk = NSA_TQ, NSA_TK
    nc = kc_ref.shape[0]
    n_chain = NSA_HEADS * tq // CHAIN
    hpc = CHAIN // tq
    t0 = qi * tq
    tok = t0 + lax.broadcasted_iota(jnp.int32, (1, tq), 1)
    tok_c = jnp.concatenate([tok] * hpc, axis=1)
    qc = _group_queries_t(qat_ref[...])
    qr = _group_queries_t(qart_ref[...])
    chain_cols = [slice(c * CHAIN, (c + 1) * CHAIN) for c in range(n_chain)]
    chain_group = [(c * hpc) // HPG for c in range(n_chain)]

    seq = kslc_ref.shape[0]
    tok2 = jnp.concatenate([tok] * NSA_GROUPS, axis=1)
    has_key = tok_c >= CMP_BLOCK - 1
    variant = (CAUSAL_VARIANTS * (t0 + tq) + seq - 1) // seq - 1

    def cmp_topk(rows_c, rows_b, forced_distinct):
        cmp_end = lax.broadcasted_iota(jnp.int32, (rows_c, 1), 0) * CMP_STRIDE + (CMP_BLOCK - 1)
        mask_c = cmp_end <= tok_c
        vo = jnp.concatenate([_with_ones_rows(vct_ref[:, :rows_c]), ovt_ref[:rows_b, :rows_c]], axis=0)
        imp_g = [None] * NSA_GROUPS

        def cmp_softmax(c, s):
            s = jnp.where(mask_c, s, NEG)
            return jnp.exp2(s - jnp.max(s, axis=0, keepdims=True)).astype(MXU_DTYPE)

        def cmp_pv(c, p):
            r = _dot(vo, p)
            inv = jnp.where(has_key, 1.0 / r[LANE:LANE + 1], 0.0)
            imp_c = r[LANE + ONES_ROWS:] * inv
            for i in range(hpc):
                g = chain_group[c]
                part = imp_c[:, i * tq:(i + 1) * tq]
                imp_g[g] = part if imp_g[g] is None else imp_g[g] + part
            return r[:LANE] * inv

        ocmp_ref[...] = jnp.concatenate(_pipelined_chains(
            n_chain, lambda c: _dot(kc_ref[:rows_c, :], qc[:, chain_cols[c]]), cmp_softmax, cmp_pv), axis=1)
        imp = jnp.concatenate(imp_g, axis=1)

        blk = lax.broadcasted_iota(jnp.int32, (rows_b, 1), 0)
        blk_f = blk.astype(jnp.float32)
        cur = tok2 // SLC_BLOCK
        forced = (blk == 0) | (blk == cur) | (blk == cur - 1)
        score = jnp.where(blk * SLC_BLOCK <= tok2, imp, -BIG)
        if forced_distinct:
            rounds = SLC_TOPK - 3
            bias = jnp.where(forced, 0.0, NEG)
            score = jnp.where(forced, REMOVED, score)
        else:
            rounds = SLC_TOPK
            bias = jnp.full(score.shape, NEG, jnp.float32)
            score = jnp.where(forced, BIG, score)
        for _ in range(rounds):
            best = jnp.max(score, axis=0, keepdims=True)
            first = jnp.min(jnp.where(score == best, blk_f, float(NSP)), axis=0, keepdims=True)
            pick = blk_f == first
            bias = jnp.where(pick, 0.0, bias)
            score = jnp.where(pick, REMOVED, score)
        bias_ref[:rows_b, :] = bias.astype(MXU_DTYPE)
        if rows_b < NSP:
            bias_ref[rows_b:, :] = jnp.full((NSP - rows_b, NSA_GROUPS * tq), NEG, MXU_DTYPE)

    for v in range(CAUSAL_VARIANTS):
        rows_c = nc * (v + 1) // CAUSAL_VARIANTS
        rows_b = min(NSP, -(-(seq // SLC_BLOCK * (v + 1) // CAUSAL_VARIANTS) // 32) * 32)
        min_t0 = seq * v // CAUSAL_VARIANTS - tq + 1
        pl.when(variant == v)(functools.partial(cmp_topk, rows_c, rows_b, v > 0 and min_t0 >= 2 * SLC_BLOCK))
    o_cmp = ocmp_ref[...]
    bias = bias_ref[...]

    qa = []
    for c in range(n_chain):
        g = chain_group[c]
        bias_c = jnp.concatenate([bias[:, g * tq:(g + 1) * tq]] * hpc, axis=1)
        qa.append(jnp.concatenate([qr[:, chain_cols[c]], bias_c], axis=0))
    m_ref[...] = jnp.full(m_ref.shape, NEG, jnp.float32)
    acc_ref[...] = jnp.zeros(acc_ref.shape, jnp.float32)
    cpt = tk // KCH

    def slc_qk(j, c, nk=tk):
        k0 = pl.multiple_of(j * tk, tk)
        ka = jnp.concatenate([kslc_ref[pl.ds(k0, nk), :], e_ref[pl.ds(k0, nk), :]], axis=1)
        return _dot(ka, qa[c])

    def slc_tile(j, cur_ref, nxt_ref, masked, nk=tk):
        vt = _with_ones_rows(jnp.concatenate([vslct_ref[j * cpt + c] for c in range(nk // KCH)], axis=1))
        kpos = j * tk + lax.broadcasted_iota(jnp.int32, (nk, 1), 0)
        m_all, acc_all = m_ref[...], acc_ref[...]
        m_out, acc_out = [], []
        pending = None
        for c in range(n_chain):
            cols = chain_cols[c]
            if nxt_ref is not None:
                nxt_ref[c] = slc_qk(jnp.maximum(j - 1, 0), c)
            s = cur_ref[c, :nk, :]
            if masked:
                s = jnp.where(kpos <= tok_c, s, NEG)
            m_old = m_all[:, cols]
            m_new = jnp.maximum(m_old, jnp.max(s, axis=0, keepdims=True))
            alpha = jnp.exp2(m_old - m_new)
            p = jnp.exp2(s - m_new)
            m_out.append(m_new)
            if pending is not None:
                acc_out.append(pending[0] + _dot(vt, pending[1]))
            pending = (acc_all[:, cols] * alpha, p.astype(MXU_DTYPE))
        acc_out.append(pending[0] + _dot(vt, pending[1]))
        m_ref[...] = jnp.concatenate(m_out, axis=1)
        acc_ref[...] = jnp.concatenate(acc_out, axis=1)

    last = t0 // tk

    def diag_tile(nk):
        for c in range(n_chain):
            sa_ref[c, :nk, :] = slc_qk(last, c, nk)
        slc_tile(last, sa_ref, sb_ref, True, nk)

    short = t0 - last * tk + tq <= tk // 2
    pl.when(short)(functools.partial(diag_tile, tk // 2))
    pl.when(jnp.logical_not(short))(functools.partial(diag_tile, tk))

    _visit_tiles(last, slc_tile, sa_ref, sb_ref)

    o_slc = _normalized(acc_ref[...])

    nwc = (WINDOW + tq) // KCH
    c0 = jnp.maximum(qi * (tq // KCH) - WINDOW // KCH, 0)
    w0 = pl.multiple_of(c0 * KCH, KCH)
    kpos = w0 + lax.broadcasted_iota(jnp.int32, (WINDOW + tq, 1), 0)
    valid_w = (kpos <= tok_c) & (kpos > tok_c - WINDOW)
    vt_w = _with_ones_rows(jnp.concatenate([vwint_ref[c0 + c] for c in range(nwc)], axis=1))

    def win_softmax(c, s):
        s = jnp.where(valid_w, s, NEG)
        return jnp.exp2(s - jnp.max(s, axis=0, keepdims=True)).astype(MXU_DTYPE)

    o_win = jnp.concatenate(_pipelined_chains(
        n_chain, lambda c: _dot(kwin_ref[pl.ds(w0, WINDOW + tq), :], qr[:, chain_cols[c]]), win_softmax,
        lambda c, p: _normalized(_dot(vt_w, p))), axis=1)

    gt = gt_ref[...]
    outs = []
    for h in range(NSA_HEADS):
        rows = slice((h // HPG) * HEAD_DIM, (h // HPG + 1) * HEAD_DIM)
        cols = slice(h * tq, (h + 1) * tq)
        outs.append(gt[3 * h:3 * h + 1] * o_cmp[rows, cols] + gt[3 * h + 1:3 * h + 2] * o_slc[rows, cols]
                    + gt[3 * h + 2:3 * h + 3] * o_win[rows, cols])
    o_ref[...] = jnp.concatenate(outs, axis=0).T.astype(o_ref.dtype)


def _nsa(bsz, seq, qat, qart, gt, kc, vct, kslc, vslct, kwin, vwint, ovt, onehot):
    n = bsz * seq
    nq = seq // NSA_TQ
    nc = kc.shape[1]
    n_chain = NSA_HEADS * NSA_TQ // CHAIN
    qspec = lambda r: pl.BlockSpec((r, NSA_TQ), lambda b, i: (0, b * nq + i))
    kspec = pl.BlockSpec((seq, LANE), lambda b, i: (b, 0))
    vspec = pl.BlockSpec((seq // KCH, LANE, KCH), lambda b, i: (b, 0, 0))
    return pl.pallas_call(
        _nsa_kernel,
        grid=(bsz, nq),
        in_specs=[qspec(512), qspec(512), qspec(G_ROWS),
                  pl.BlockSpec((None, nc, LANE), lambda b, i: (b, 0, 0)),
                  pl.BlockSpec((None, LANE, nc), lambda b, i: (b, 0, 0)),
                  kspec, vspec, kspec, vspec, _const_spec(ovt.shape), _const_spec(onehot.shape)],
        out_specs=pl.BlockSpec((NSA_TQ, 512), lambda b, i: (b * nq + i, 0)),
        out_shape=jax.ShapeDtypeStruct((n, 512), MXU_DTYPE),
        scratch_shapes=[pltpu.VMEM((1, NSA_HEADS * NSA_TQ), jnp.float32),
                        pltpu.VMEM((LANE + ONES_ROWS, NSA_HEADS * NSA_TQ), jnp.float32),
                        pltpu.VMEM((n_chain, NSA_TK, CHAIN), jnp.float32),
                        pltpu.VMEM((n_chain, NSA_TK, CHAIN), jnp.float32),
                        pltpu.VMEM((LANE, NSA_HEADS * NSA_TQ), jnp.float32),
                        pltpu.VMEM((NSP, NSA_GROUPS * NSA_TQ), MXU_DTYPE)],
        compiler_params=_params("parallel", "parallel"),
        name="nsa_attention",
    )(qat, qart, gt, kc, vct, kslc, vslct, kwin, vwint, ovt, onehot)


def _diff_kernel(lambda_init, q_ref, k_ref, vt_ref, lq1_ref, lk1_ref, lq2_ref, lk2_ref, subln_ref,
                 o_ref, m_ref, acc_ref, sa_ref, sb_ref):
    qi = pl.program_id(2)
    tq, tk = DIFF_TQ, DIFF_TK
    cpt = tk // KCH
    t0 = qi * tq
    qt = q_ref[...]
    z = jnp.zeros((HEAD_DIM, tq), qt.dtype)
    qp = jnp.concatenate([jnp.concatenate([qt[:HEAD_DIM], z], axis=0),
                          jnp.concatenate([z, qt[HEAD_DIM:]], axis=0)], axis=1)
    tok = t0 + lax.broadcasted_iota(jnp.int32, (1, CHAIN), 1)

    m_ref[...] = jnp.full(m_ref.shape, NEG, jnp.float32)
    acc_ref[...] = jnp.zeros(acc_ref.shape, jnp.float32)

    n_chain = 2 * tq // CHAIN

    def qk(j, c):
        k = k_ref[pl.ds(pl.multiple_of(j * tk, tk), tk), :]
        return _dot(k, qp[:, c * CHAIN:(c + 1) * CHAIN])

    def tile(j, cur_ref, nxt_ref, masked):
        vt = _with_ones_rows(jnp.concatenate([vt_ref[j * cpt + c] for c in range(cpt)], axis=1))
        kpos = j * tk + lax.broadcasted_iota(jnp.int32, (tk, 1), 0)
        m_all, acc_all = m_ref[...], acc_ref[...]
        m_out, acc_out = [], []
        pending = None
        for c in range(n_chain):
            cols = slice(c * CHAIN, (c + 1) * CHAIN)
            if nxt_ref is not None:
                nxt_ref[c] = qk(jnp.maximum(j - 1, 0), c)
            s = cur_ref[c]
            if masked:
                s = jnp.where(kpos <= tok + (c * CHAIN) % tq, s, NEG)
            m_old = m_all[:, cols]
            m_new = jnp.maximum(m_old, jnp.max(s, axis=0, keepdims=True))
            alpha = jnp.exp2(m_old - m_new)
            p = jnp.exp2(s - m_new)
            m_out.append(m_new)
            if pending is not None:
                acc_out.append(pending[0] + _dot(vt, pending[1]))
            pending = (acc_all[:, cols] * alpha, p.astype(MXU_DTYPE))
        acc_out.append(pending[0] + _dot(vt, pending[1]))
        m_ref[...] = jnp.concatenate(m_out, axis=1)
        acc_ref[...] = jnp.concatenate(acc_out, axis=1)

    last = t0 // tk
    for c in range(n_chain):
        sa_ref[c] = qk(last, c)
    tile(last, sa_ref, sb_ref, True)

    _visit_tiles(last, tile, sa_ref, sb_ref)

    lam = (jnp.exp(jnp.sum(lq1_ref[...] * lk1_ref[...], axis=1, keepdims=True))
           - jnp.exp(jnp.sum(lq2_ref[...] * lk2_ref[...], axis=1, keepdims=True)) + lambda_init)
    o = _normalized(acc_ref[...])
    o = o[:, :tq] - lam * o[:, tq:]
    o = o * lax.rsqrt(jnp.mean(o * o, axis=0, keepdims=True) + NORM_EPS)
    o = o * subln_ref[...] * (1.0 - lambda_init)
    o_ref[...] = o.T.astype(o_ref.dtype)


def _diff(bsz, seq, lambda_init, qbt, kb, vbt, lq1, lk1, lq2, lk2, subln):
    n = bsz * seq
    nq = seq // DIFF_TQ
    vec = _const_spec((1, HEAD_DIM))
    return pl.pallas_call(
        functools.partial(_diff_kernel, lambda_init),
        grid=(bsz, DIFF_HEADS, nq),
        in_specs=[pl.BlockSpec((DIFF_V_DIM, DIFF_TQ), lambda b, h, i: (h, b * nq + i)),
                  pl.BlockSpec((seq, LANE), lambda b, h, i: (b, h)),
                  pl.BlockSpec((seq // KCH, DIFF_V_DIM, KCH), lambda b, h, i: (b, h, 0)),
                  vec, vec, vec, vec, _const_spec((DIFF_V_DIM, 1))],
        out_specs=pl.BlockSpec((DIFF_TQ, DIFF_V_DIM), lambda b, h, i: (b * nq + i, h)),
        out_shape=jax.ShapeDtypeStruct((n, DIFF_HEADS * DIFF_V_DIM), MXU_DTYPE),
        scratch_shapes=[pltpu.VMEM((1, 2 * DIFF_TQ), jnp.float32),
                        pltpu.VMEM((DIFF_V_DIM + ONES_ROWS, 2 * DIFF_TQ), jnp.float32),
                        pltpu.VMEM((2 * DIFF_TQ // CHAIN, DIFF_TK, CHAIN), jnp.float32),
                        pltpu.VMEM((2 * DIFF_TQ // CHAIN, DIFF_TK, CHAIN), jnp.float32)],
        compiler_params=_params("parallel", "parallel", "parallel"),
        name="diff_attention",
    )(qbt, kb, vbt, lq1, lk1, lq2, lk2, subln)


def _merge_kernel(x_ref, g_ref, wm_ref, bm_ref, oa_ref, ob_ref, oc_ref, wa_ref, wb_ref, wc_ref, wo_ref, o_ref):
    x = x_ref[...]
    d = x.shape[1]
    hb = _rmsnorm_rows(x, g_ref[...]).astype(MXU_DTYPE)
    mixed = None
    for i, (br_ref, w_ref) in enumerate(((oa_ref, wa_ref), (ob_ref, wb_ref), (oc_ref, wc_ref))):
        cols = slice(i * d, (i + 1) * d)
        gate = jax.nn.sigmoid(_dot(hb, wm_ref[:, cols]) + bm_ref[:, cols])
        term = gate * _dot(br_ref[...], w_ref[...])
        mixed = term if mixed is None else mixed + term
    o_ref[...] = x + _dot(mixed.astype(MXU_DTYPE), wo_ref[...])


def _merge(x2, g, wm, bm, oa, ob, oc, wa, wb, wc, wo):
    n, d = x2.shape
    row = lambda w: pl.BlockSpec((TM, w), lambda i: (i, 0))
    return pl.pallas_call(
        _merge_kernel,
        grid=(n // TM,),
        in_specs=[row(d), _const_spec(g.shape), _const_spec(wm.shape), _const_spec(bm.shape),
                  row(BRANCH), row(BRANCH), row(BRANCH),
                  _const_spec(wa.shape), _const_spec(wb.shape), _const_spec(wc.shape), _const_spec(wo.shape)],
        out_specs=row(d),
        out_shape=jax.ShapeDtypeStruct((n, d), jnp.float32),
        compiler_params=_params("parallel"),
        name="merge_out",
    )(x2, g, wm, bm, oa, ob, oc, wa, wb, wc, wo)


def _ffn_kernel(final, n_chunks, x_ref, g_ref, w1_ref, w3_ref, w2_ref, gf_ref, o_ref):
    x = x_ref[...]
    hb = _rmsnorm_rows(x, g_ref[...]).astype(MXU_DTYPE)
    cw = w1_ref.shape[1] // n_chunks
    y = x
    for c in range(n_chunks):
        cols = slice(c * cw, (c + 1) * cw)
        act = jax.nn.silu(_dot(hb, w1_ref[:, cols])) * _dot(hb, w3_ref[:, cols])
        y = y + _dot(act.astype(MXU_DTYPE), w2_ref[cols, :])
    if final:
        y = _rmsnorm_rows(y, gf_ref[...])
    o_ref[...] = y


def _ffn(x2, g, w1, w3, w2, gf, final):
    n, d = x2.shape
    dff = w1.shape[1]
    n_chunks = 2 if dff % (2 * LANE) == 0 else 1
    row = pl.BlockSpec((TM, d), lambda i: (i, 0))
    return pl.pallas_call(
        functools.partial(_ffn_kernel, final, n_chunks),
        grid=(n // TM,),
        in_specs=[row, _const_spec(g.shape), _const_spec(w1.shape), _const_spec(w3.shape),
                  _const_spec(w2.shape), _const_spec(gf.shape)],
        out_specs=row,
        out_shape=jax.ShapeDtypeStruct((n, d), jnp.float32),
        compiler_params=_params("parallel"),
        name="ffn",
    )(x2, g, w1, w3, w2, gf)


def _overlap_t(seq):
    nc = seq // CMP_STRIDE
    c_start = jnp.arange(nc)[None, :] * CMP_STRIDE
    s_start = jnp.arange(NSP)[:, None] * SLC_BLOCK
    ov = (c_start < s_start + SLC_BLOCK) & (c_start + CMP_BLOCK > s_start)
    return ov.astype(MXU_DTYPE)


KV_WIDTH = NSA_GROUPS * HEAD_DIM
IN_SPLITS = (("q_a", NSA_HEADS * HEAD_DIM), ("k_cmp", KV_WIDTH), ("v_cmp", KV_WIDTH), ("k_slc", KV_WIDTH),
             ("v_slc", KV_WIDTH), ("k_win", KV_WIDTH), ("v_win", KV_WIDTH), ("g_a", 3 * NSA_HEADS),
             ("q_b", 2 * DIFF_HEADS * HEAD_DIM), ("k_b", 2 * DIFF_HEADS * HEAD_DIM),
             ("v_b", DIFF_HEADS * DIFF_V_DIM), ("uv_c", 2 * SGU_WIDTH))


def _split_in_proj(w):
    cols, off = {}, 0
    for name, width in IN_SPLITS:
        cols[name] = w[:, off:off + width]
        off += width
    assert off == w.shape[1]
    return cols


def _block_onehot(seq):
    return (jnp.arange(seq)[:, None] // SLC_BLOCK == jnp.arange(NSP)[None, :]).astype(MXU_DTYPE)


def kernel(x, positions, attn_norm, w_in, cmp_pos_k, cmp_k_w1, cmp_k_w2, cmp_pos_v, cmp_v_w1, cmp_v_w2, diff_lq1, diff_lk1, diff_lq2, diff_lk2, diff_subln, sgu_norm, sgu_w, sgu_b, w_branch_a, w_branch_b, w_branch_c, w_merge, b_merge, w_out, ffn_norm, w_ffn1, w_ffn3, w_ffn2, final_norm):
    bsz, seq, d = x.shape
    depth = w_in.shape[0]
    n = bsz * seq
    assert seq % NSA_TK == 0 and seq >= WINDOW + NSA_TQ and seq // SLC_BLOCK <= NSP and n % TM == 0
    bf = MXU_DTYPE

    tables = _rope_tables(positions)
    ovt = _overlap_t(seq)
    onehot = _block_onehot(seq)
    causal = jnp.tril(jnp.ones((SGU_CHUNK, SGU_CHUNK), dtype=bool))
    x2 = x.reshape(n, d)
    row = lambda v: v.reshape(1, -1)

    for l in range(depth):
        lambda_init = 0.8 - 0.6 * math.exp(-0.3 * l)
        w = _split_in_proj(w_in[l])
        wp = jnp.concatenate([w["k_cmp"], w["v_cmp"], w["k_slc"], w["k_win"], w["k_b"], w["uv_c"]],
                             axis=1).astype(bf)
        wt = jnp.concatenate([w["q_a"], w["v_slc"], w["v_win"], w["g_a"],
                              jnp.zeros((d, G_ROWS - 3 * NSA_HEADS), x.dtype), w["q_b"], w["v_b"]],
                             axis=1).T.astype(bf)
        sgu_wm = jnp.where(causal[None], sgu_w[l], 0.0).astype(bf)
        sgu_bias = jnp.repeat(sgu_b[l].T, SGU_WIDTH // SGU_GROUPS, axis=1)

        (kcmp, vcmp, kslc, kwin, kb, oc, qat, qart, vslct, vwint, gt, qbt, vbt) = _inproj(
            x2, row(attn_norm[l]), wp, wt, tables, row(sgu_norm[l]), sgu_wm, sgu_bias)

        kc = _compress(kcmp, bsz, cmp_pos_k[l], cmp_k_w1[l], cmp_k_w2[l], transposed=False)
        vct = _compress(vcmp, bsz, cmp_pos_v[l], cmp_v_w1[l], cmp_v_w2[l], transposed=True)

        oa = _nsa(bsz, seq, qat, qart, gt, kc, vct, kslc, vslct, kwin, vwint, ovt, onehot)
        ob = _diff(bsz, seq, lambda_init, qbt, kb, vbt, row(diff_lq1[l]), row(diff_lk1[l]),
                   row(diff_lq2[l]), row(diff_lk2[l]), diff_subln[l].reshape(DIFF_V_DIM, 1))

        x2 = _merge(x2, row(attn_norm[l]), w_merge[l].astype(bf), row(b_merge[l]), oa, ob, oc,
                    w_branch_a[l].astype(bf), w_branch_b[l].astype(bf), w_branch_c[l].astype(bf),
                    w_out[l].astype(bf))
        x2 = _ffn(x2, row(ffn_norm[l]), w_ffn1[l].astype(bf), w_ffn3[l].astype(bf), w_ffn2[l].astype(bf),
                  row(final_norm), final=(l == depth - 1))
    return x2.reshape(bsz, seq, d)
```

```python
import functools
import math

import jax
import jax.numpy as jnp
from jax import lax
from jax.experimental import pallas as pl
from jax.experimental.pallas import tpu as pltpu

HEAD_DIM = 64
HALF = HEAD_DIM // 2
ROPE_THETA = 10000.0
NORM_EPS = 1e-6
BIG = 1e9
NEG = -1e30
REMOVED = -3e38

NSA_HEADS = 8
NSA_GROUPS = 2
HPG = NSA_HEADS // NSA_GROUPS
CMP_BLOCK = 32
CMP_STRIDE = 16
CMP_HIDDEN = 256
SLC_BLOCK = 64
SLC_TOPK = 16
WINDOW = 512
DIFF_HEADS = 4
DIFF_V_DIM = 2 * HEAD_DIM
SGU_CHUNK = 128
SGU_GROUPS = 4
SGU_WIDTH = 512
BRANCH = 512
SCALE = HEAD_DIM ** -0.5
QSCALE = SCALE * math.log2(math.e)
ONES_ROWS = 16

LANE = 128
NSP = 128
MXU_DTYPE = jnp.bfloat16
VMEM_LIMIT = 56 * 1024 * 1024

TM = 512
NSA_TQ = 256
NSA_TK = 512
DIFF_TQ = 512
DIFF_TK = 512
DIFF_HPS = 2
CHAIN = 256
SWEEP_UNROLL = 4
CAUSAL_VARIANTS = 4
KCH = 128


def _params(*sem):
    return pltpu.CompilerParams(dimension_semantics=sem, vmem_limit_bytes=VMEM_LIMIT)


def _const_spec(shape):
    nd = len(shape)
    return pl.BlockSpec(shape, lambda *_: (0,) * nd, pipeline_mode=pl.Buffered(1))


def _dot(a, b):
    return jnp.dot(a, b, preferred_element_type=jnp.float32)


def _dot_nt(a, b):
    return lax.dot_general(a, b, (((1,), (1,)), ((), ())), preferred_element_type=jnp.float32)


def _rmsnorm_rows(x, g):
    return x * lax.rsqrt(jnp.mean(x * x, axis=-1, keepdims=True) + NORM_EPS) * g


def _tables_kernel(pos_row_ref, f_col_ref, cos_ref, sin_ref, cost_ref, sint_ref):
    angt = f_col_ref[...] * pos_row_ref[...].astype(jnp.float32)
    cost, sint = jnp.cos(angt), jnp.sin(angt)
    cost_ref[...] = cost
    sint_ref[...] = sint
    reps = LANE // HALF
    cos_ref[...] = jnp.concatenate([cost] * reps, axis=0).T
    sin = jnp.concatenate([sint] * reps, axis=0).T
    lane = lax.broadcasted_iota(jnp.int32, sin.shape, 1)
    sin_ref[...] = jnp.where((lane % HEAD_DIM) < HALF, -sin, sin)


def _rope_tables(positions):
    n = positions.size
    inv_freq = ROPE_THETA ** (-jnp.arange(HALF, dtype=jnp.float32) / HALF)
    f_col = inv_freq.reshape(HALF, 1)
    pos_row = positions.reshape(1, n)
    return pl.pallas_call(
        _tables_kernel,
        grid=(n // TM,),
        in_specs=[pl.BlockSpec((1, TM), lambda i: (0, i)), _const_spec((HALF, 1))],
        out_specs=[pl.BlockSpec((TM, LANE), lambda i: (i, 0)),
                   pl.BlockSpec((TM, LANE), lambda i: (i, 0)),
                   pl.BlockSpec((HALF, TM), lambda i: (0, i)),
                   pl.BlockSpec((HALF, TM), lambda i: (0, i))],
        out_shape=[jax.ShapeDtypeStruct((n, LANE), jnp.float32),
                   jax.ShapeDtypeStruct((n, LANE), jnp.float32),
                   jax.ShapeDtypeStruct((HALF, n), jnp.float32),
                   jax.ShapeDtypeStruct((HALF, n), jnp.float32)],
        compiler_params=_params("parallel"),
        name="rope_tables",
    )(pos_row, f_col)


P_KVC, P_KSLC, P_KWIN, P_KB, P_UV, P_END = 0, 256, 384, 512, 1024, 2048
T_QA, T_VSLC, T_VWIN, T_G, T_QB, T_VB, T_END = 0, 512, 640, 768, 800, 1312, 1824
G_ROWS = T_QB - T_G


def _rope_plain(y, cos, sin):
    lane = lax.broadcasted_iota(jnp.int32, y.shape, 1)
    swapped = jnp.where((lane % HEAD_DIM) < HALF, pltpu.roll(y, LANE - HALF, 1), pltpu.roll(y, HALF, 1))
    return y * cos + swapped * sin


def _rope_t(y, cos, sin):
    y1, y2 = y[:HALF], y[HALF:]
    return jnp.concatenate([y1 * cos - y2 * sin, y2 * cos + y1 * sin], axis=0)


def _inproj_kernel(x_ref, g_ref, wp_ref, wt_ref, cos_ref, sin_ref, cost_ref, sint_ref,
                   sgu_g_ref, sgu_w_ref, sgu_b_ref,
                   kcmp_ref, vcmp_ref, kslc_ref, kwin_ref, kb_ref, oc_ref,
                   qat_ref, qart_ref, vslct_ref, vwint_ref, gt_ref, qbt_ref, vbt_ref):
    x = x_ref[...]
    hb = _rmsnorm_rows(x, g_ref[...]).astype(MXU_DTYPE)
    cos, sin = cos_ref[...], sin_ref[...]
    cost, sint = cost_ref[...], sint_ref[...]
    tm = x.shape[0]

    z = jax.nn.gelu(_dot(hb, wp_ref[:, P_UV:P_END]))
    u = z[:, :SGU_WIDTH]
    vn = _rmsnorm_rows(z[:, SGU_WIDTH:], sgu_g_ref[...]).astype(MXU_DTYPE)

    kvc = _dot(hb, wp_ref[:, P_KVC:P_KSLC])
    kcmp_ref[...] = kvc[:, :LANE]
    vcmp_ref[...] = kvc[:, LANE:]
    kslc_ref[...] = _rope_plain(_dot(hb, wp_ref[:, P_KSLC:P_KWIN]), cos, sin).astype(kslc_ref.dtype)
    kwin_ref[...] = _rope_plain(_dot(hb, wp_ref[:, P_KWIN:P_KB]), cos, sin).astype(kwin_ref.dtype)
    kb = _dot(hb, wp_ref[:, P_KB:P_UV])
    for c in range((P_UV - P_KB) // LANE):
        sl = slice(c * LANE, (c + 1) * LANE)
        kb_ref[:, sl] = _rope_plain(kb[:, sl], cos, sin).astype(kb_ref.dtype)

    qat = _dot_nt(wt_ref[T_QA:T_VSLC, :], hb) * QSCALE
    qat_ref[...] = qat.astype(qat_ref.dtype)
    for h in range(NSA_HEADS):
        sl = slice(h * HEAD_DIM, (h + 1) * HEAD_DIM)
        qart_ref[sl, :] = _rope_t(qat[sl], cost, sint).astype(qart_ref.dtype)
    vst = _dot_nt(wt_ref[T_VSLC:T_VWIN, :], hb).astype(vslct_ref.dtype)
    vwt = _dot_nt(wt_ref[T_VWIN:T_G, :], hb).astype(vwint_ref.dtype)
    for c in range(tm // KCH):
        sl = slice(c * KCH, (c + 1) * KCH)
        vslct_ref[c] = vst[:, sl]
        vwint_ref[c] = vwt[:, sl]
    gt_ref[...] = jax.nn.sigmoid(_dot_nt(wt_ref[T_G:T_QB, :], hb))
    qbt = _dot_nt(wt_ref[T_QB:T_VB, :], hb) * QSCALE
    for h in range(2 * DIFF_HEADS):
        sl = slice(h * HEAD_DIM, (h + 1) * HEAD_DIM)
        qbt_ref[sl, :] = _rope_t(qbt[sl], cost, sint).astype(qbt_ref.dtype)
    vbt = _dot_nt(wt_ref[T_VB:T_END, :], hb).astype(vbt_ref.dtype)
    for c in range(tm // KCH):
        vbt_ref[c] = vbt[:, c * KCH:(c + 1) * KCH]

    gdim = SGU_WIDTH // SGU_GROUPS
    for ci in range(tm // SGU_CHUNK):
        rows = slice(ci * SGU_CHUNK, (ci + 1) * SGU_CHUNK)
        for gi in range(SGU_GROUPS):
            cols = slice(gi * gdim, (gi + 1) * gdim)
            s = _dot(sgu_w_ref[gi], vn[rows, cols]) + sgu_b_ref[:, cols]
            oc_ref[rows, cols] = (u[rows, cols] * s).astype(oc_ref.dtype)


def _inproj(x2, g, wp, wt, tables, sgu_g, sgu_w, sgu_b):
    n, d = x2.shape
    cos, sin, cost, sint = tables
    row = lambda w: pl.BlockSpec((TM, w), lambda i: (i, 0))
    colt = lambda r: pl.BlockSpec((r, TM), lambda i: (0, i))
    chunk = lambda r: pl.BlockSpec((TM // KCH, r, KCH), lambda i: (i, 0, 0))
    bf = MXU_DTYPE
    outs = [
        (row(LANE), jax.ShapeDtypeStruct((n, LANE), jnp.float32)),
        (row(LANE), jax.ShapeDtypeStruct((n, LANE), jnp.float32)),
        (row(LANE), jax.ShapeDtypeStruct((n, LANE), bf)),
        (row(LANE), jax.ShapeDtypeStruct((n, LANE), bf)),
        (row(512), jax.ShapeDtypeStruct((n, 512), bf)),
        (row(SGU_WIDTH), jax.ShapeDtypeStruct((n, SGU_WIDTH), bf)),
        (colt(512), jax.ShapeDtypeStruct((512, n), bf)),
        (colt(512), jax.ShapeDtypeStruct((512, n), bf)),
        (chunk(LANE), jax.ShapeDtypeStruct((n // KCH, LANE, KCH), bf)),
        (chunk(LANE), jax.ShapeDtypeStruct((n // KCH, LANE, KCH), bf)),
        (colt(G_ROWS), jax.ShapeDtypeStruct((G_ROWS, n), jnp.float32)),
        (colt(512), jax.ShapeDtypeStruct((512, n), bf)),
        (chunk(512), jax.ShapeDtypeStruct((n // KCH, 512, KCH), bf)),
    ]
    return pl.pallas_call(
        _inproj_kernel,
        grid=(n // TM,),
        in_specs=[row(d), _const_spec((1, d)), _const_spec(wp.shape), _const_spec(wt.shape),
                  row(LANE), row(LANE), colt(HALF), colt(HALF),
                  _const_spec(sgu_g.shape), _const_spec(sgu_w.shape), _const_spec(sgu_b.shape)],
        out_specs=[o[0] for o in outs],
        out_shape=[o[1] for o in outs],
        compiler_params=_params("parallel"),
        name="inproj",
    )(x2, g, wp, wt, cos, sin, cost, sint, sgu_g, sgu_w, sgu_b)


def _chunked_tokens(x_ref, g, nc):
    lanes = slice(g * HEAD_DIM, (g + 1) * HEAD_DIM)
    return jnp.concatenate([x_ref[pl.ds(t, nc, stride=CMP_STRIDE), :][:, lanes] for t in range(CMP_STRIDE)],
                           axis=1)


def _compress_k_kernel(x_ref, pa_ref, pb_ref, w1a_ref, w1b_ref, w2_ref, o_ref):
    nc = o_ref.shape[0]
    acc = jnp.zeros(o_ref.shape, jnp.float32)
    for g in range(NSA_GROUPS):
        xg = _chunked_tokens(x_ref, g, nc)
        a = _dot((xg + pa_ref[...]).astype(MXU_DTYPE), w1a_ref[...])
        b = _dot((xg + pb_ref[...]).astype(MXU_DTYPE), w1b_ref[...])
        hid = jax.nn.gelu(a + pltpu.roll(b, nc - 1, 0))
        acc = acc + _dot(hid.astype(MXU_DTYPE), w2_ref[g])
    o_ref[...] = acc.astype(o_ref.dtype)


def _compress_v_kernel(x_ref, pa_ref, pb_ref, w1at_ref, w1bt_ref, w2t_ref, o_ref):
    nc = o_ref.shape[1]
    acc = jnp.zeros(o_ref.shape, jnp.float32)
    for g in range(NSA_GROUPS):
        xg = _chunked_tokens(x_ref, g, nc)
        a = _dot_nt(w1at_ref[...], (xg + pa_ref[...]).astype(MXU_DTYPE))
        b = _dot_nt(w1bt_ref[...], (xg + pb_ref[...]).astype(MXU_DTYPE))
        hid = jax.nn.gelu(a + pltpu.roll(b, nc - 1, 1))
        acc = acc + _dot(w2t_ref[g], hid.astype(MXU_DTYPE))
    o_ref[...] = acc.astype(o_ref.dtype)


def _compress(kvc, b, pos, w1, w2, transposed):
    seq = kvc.shape[0] // b
    nc, cw = seq // CMP_STRIDE, CMP_STRIDE * HEAD_DIM
    pos_flat = pos.reshape(1, CMP_BLOCK * HEAD_DIM)
    pa, pb = pos_flat[:, :cw], pos_flat[:, cw:]
    w1a, w1b = w1[:cw].astype(MXU_DTYPE), w1[cw:].astype(MXU_DTYPE)
    w2p = jnp.zeros((NSA_GROUPS, CMP_HIDDEN, LANE), jnp.float32)
    for g in range(NSA_GROUPS):
        w2p = w2p.at[g, :, g * HEAD_DIM:(g + 1) * HEAD_DIM].set(w2)
    w2p = w2p.astype(MXU_DTYPE)
    x_spec = pl.BlockSpec((seq, kvc.shape[1]), lambda i: (i, 0))
    if not transposed:
        return pl.pallas_call(
            _compress_k_kernel, grid=(b,),
            in_specs=[x_spec, _const_spec(pa.shape), _const_spec(pb.shape), _const_spec(w1a.shape),
                      _const_spec(w1b.shape), _const_spec(w2p.shape)],
            out_specs=pl.BlockSpec((None, nc, LANE), lambda i: (i, 0, 0)),
            out_shape=jax.ShapeDtypeStruct((b, nc, LANE), MXU_DTYPE),
            compiler_params=_params("parallel"), name="compress_k",
        )(kvc, pa, pb, w1a, w1b, w2p)
    w1at, w1bt, w2pt = w1a.T, w1b.T, jnp.swapaxes(w2p, 1, 2)
    return pl.pallas_call(
        _compress_v_kernel, grid=(b,),
        in_specs=[x_spec, _const_spec(pa.shape), _const_spec(pb.shape), _const_spec(w1at.shape),
                  _const_spec(w1bt.shape), _const_spec(w2pt.shape)],
        out_specs=pl.BlockSpec((None, LANE, nc), lambda i: (i, 0, 0)),
        out_shape=jax.ShapeDtypeStruct((b, LANE, nc), MXU_DTYPE),
        compiler_params=_params("parallel"), name="compress_v",
    )(kvc, pa, pb, w1at, w1bt, w2pt)


def _group_queries_t(qt):
    z = jnp.zeros((HEAD_DIM, qt.shape[1]), qt.dtype)
    cols = []
    for h in range(NSA_HEADS):
        slab = qt[h * HEAD_DIM:(h + 1) * HEAD_DIM]
        cols.append(jnp.concatenate([slab, z] if h < HPG else [z, slab], axis=0))
    return jnp.concatenate(cols, axis=1)


def _with_ones_rows(vt):
    return jnp.concatenate([vt, jnp.ones((ONES_ROWS, vt.shape[1]), vt.dtype)], axis=0)


def _normalized(acc):
    dims = acc.shape[0] - ONES_ROWS
    return acc[:dims] * (1.0 / acc[dims:dims + 1])


def _visit_tiles(last, tile, sa_ref, sb_ref):
    def slots(u):
        return (sb_ref, sa_ref) if u % 2 == 0 else (sa_ref, sb_ref)

    def group(i, carry):
        j = last - 1 - SWEEP_UNROLL * i
        for u in range(SWEEP_UNROLL):
            tile(j - u, *slots(u), False)
        return carry

    lax.fori_loop(0, last // SWEEP_UNROLL, group, 0)
    rem = last % SWEEP_UNROLL
    for u in range(SWEEP_UNROLL - 1):
        cur_ref, nxt_ref = slots(u)
        pl.when(rem == u + 1)(functools.partial(tile, 0, cur_ref, None, False))
        if u < SWEEP_UNROLL - 2:
            pl.when(rem > u + 1)(functools.partial(tile, rem - 1 - u, cur_ref, nxt_ref, False))


def _pipelined_chains(n_chain, qk, softmax, pv):
    outs = []
    s_next = qk(0)
    pending = None
    for c in range(n_chain):
        s = s_next
        if c + 1 < n_chain:
            s_next = qk(c + 1)
        r = softmax(c, s)
        if pending is not None:
            outs.append(pv(*pending))
        pending = (c, r)
    outs.append(pv(*pending))
    return outs


def _nsa_kernel(qat_ref, qart_ref, gt_ref, kc_ref, vct_ref, kslc_ref, vslct_ref, kwin_ref, vwint_ref,
                ovt_ref, e_ref, o_ref, m_ref, acc_ref, sa_ref, sb_ref, ocmp_ref, bias_ref):
    qi = pl.program_id(1)
    tq, tk = NSA_TQ, NSA_TK
    nc = kc_ref.shape[0]
    n_chain = NSA_HEADS * tq // CHAIN
    hpc = CHAIN // tq
    t0 = qi * tq
    tok = t0 + lax.broadcasted_iota(jnp.int32, (1, tq), 1)
    tok_c = jnp.concatenate([tok] * hpc, axis=1)
    qc = _group_queries_t(qat_ref[...])
    qr = _group_queries_t(qart_ref[...])
    chain_cols = [slice(c * CHAIN, (c + 1) * CHAIN) for c in range(n_chain)]
    chain_group = [(c * hpc) // HPG for c in range(n_chain)]

    seq = kslc_ref.shape[0]
    tok2 = jnp.concatenate([tok] * NSA_GROUPS, axis=1)
    has_key = tok_c >= CMP_BLOCK - 1
    variant = (CAUSAL_VARIANTS * (t0 + tq) + seq - 1) // seq - 1

    def cmp_topk(rows_c, rows_b, forced_distinct):
        cmp_end = lax.broadcasted_iota(jnp.int32, (rows_c, 1), 0) * CMP_STRIDE + (CMP_BLOCK - 1)
        mask_c = cmp_end <= tok_c
        vo = jnp.concatenate([_with_ones_rows(vct_ref[:, :rows_c]), ovt_ref[:rows_b, :rows_c]], axis=0)
        imp_g = [None] * NSA_GROUPS

        def cmp_softmax(c, s):
            s = jnp.where(mask_c, s, NEG)
            return jnp.exp2(s - jnp.max(s, axis=0, keepdims=True)).astype(MXU_DTYPE)

        def cmp_pv(c, p):
            r = _dot(vo, p)
            inv = jnp.where(has_key, 1.0 / r[LANE:LANE + 1], 0.0)
            imp_c = r[LANE + ONES_ROWS:] * inv
            for i in range(hpc):
                g = chain_group[c]
                part = imp_c[:, i * tq:(i + 1) * tq]
                imp_g[g] = part if imp_g[g] is None else imp_g[g] + part
            return r[:LANE] * inv

        ocmp_ref[...] = jnp.concatenate(_pipelined_chains(
            n_chain, lambda c: _dot(kc_ref[:rows_c, :], qc[:, chain_cols[c]]), cmp_softmax, cmp_pv), axis=1)
        imp = jnp.concatenate(imp_g, axis=1)

        blk = lax.broadcasted_iota(jnp.int32, (rows_b, 1), 0)
        blk_f = blk.astype(jnp.float32)
        cur = tok2 // SLC_BLOCK
        forced = (blk == 0) | (blk == cur) | (blk == cur - 1)
        score = jnp.where(blk * SLC_BLOCK <= tok2, imp, -BIG)
        if forced_distinct:
            rounds = SLC_TOPK - 3
            bias = jnp.where(forced, 0.0, NEG)
            score = jnp.where(forced, REMOVED, score)
        else:
            rounds = SLC_TOPK
            bias = jnp.full(score.shape, NEG, jnp.float32)
            score = jnp.where(forced, BIG, score)
        for _ in range(rounds):
            best = jnp.max(score, axis=0, keepdims=True)
            first = jnp.min(jnp.where(score == best, blk_f, float(NSP)), axis=0, keepdims=True)
            pick = blk_f == first
            bias = jnp.where(pick, 0.0, bias)
            score = jnp.where(pick, REMOVED, score)
        bias_ref[:rows_b, :] = bias.astype(MXU_DTYPE)
        if rows_b < NSP:
            bias_ref[rows_b:, :] = jnp.full((NSP - rows_b, NSA_GROUPS * tq), NEG, MXU_DTYPE)

    for v in range(CAUSAL_VARIANTS):
        rows_c = nc * (v + 1) // CAUSAL_VARIANTS
        rows_b = min(NSP, -(-(seq // SLC_BLOCK * (v + 1) // CAUSAL_VARIANTS) // 32) * 32)
        min_t0 = seq * v // CAUSAL_VARIANTS - tq + 1
        pl.when(variant == v)(functools.partial(cmp_topk, rows_c, rows_b, v > 0 and min_t0 >= 2 * SLC_BLOCK))
    o_cmp = ocmp_ref[...]
    bias = bias_ref[...]

    qa = []
    for c in range(n_chain):
        g = chain_group[c]
        bias_c = jnp.concatenate([bias[:, g * tq:(g + 1) * tq]] * hpc, axis=1)
        qa.append(jnp.concatenate([qr[:, chain_cols[c]], bias_c], axis=0))
    m_ref[...] = jnp.full(m_ref.shape, NEG, jnp.float32)
    acc_ref[...] = jnp.zeros(acc_ref.shape, jnp.float32)
    cpt = tk // KCH

    def slc_qk(j, c, nk=tk):
        k0 = pl.multiple_of(j * tk, tk)
        ka = jnp.concatenate([kslc_ref[pl.ds(k0, nk), :], e_ref[pl.ds(k0, nk), :]], axis=1)
        return _dot(ka, qa[c])

    def slc_tile(j, cur_ref, nxt_ref, masked, nk=tk):
        vt = _with_ones_rows(jnp.concatenate([vslct_ref[j * cpt + c] for c in range(nk // KCH)], axis=1))
        kpos = j * tk + lax.broadcasted_iota(jnp.int32, (nk, 1), 0)
        m_all, acc_all = m_ref[...], acc_ref[...]
        m_out, acc_out = [], []
        pending = None
        for c in range(n_chain):
            cols = chain_cols[c]
            if nxt_ref is not None:
                nxt_ref[c] = slc_qk(jnp.maximum(j - 1, 0), c)
            s = cur_ref[c, :nk, :]
            if masked:
                s = jnp.where(kpos <= tok_c, s, NEG)
            m_old = m_all[:, cols]
            m_new = jnp.maximum(m_old, jnp.max(s, axis=0, keepdims=True))
            alpha = jnp.exp2(m_old - m_new)
            p = jnp.exp2(s - m_new)
            m_out.append(m_new)
            if pending is not None:
                acc_out.append(pending[0] + _dot(vt, pending[1]))
            pending = (acc_all[:, cols] * alpha, p.astype(MXU_DTYPE))
        acc_out.append(pending[0] + _dot(vt, pending[1]))
        m_ref[...] = jnp.concatenate(m_out, axis=1)
        acc_ref[...] = jnp.concatenate(acc_out, axis=1)

    last = t0 // tk

    def diag_tile(nk):
        for c in range(n_chain):
            sa_ref[c, :nk, :] = slc_qk(last, c, nk)
        slc_tile(last, sa_ref, sb_ref, True, nk)

    short = t0 - last * tk + tq <= tk // 2
    pl.when(short)(functools.partial(diag_tile, tk // 2))
    pl.when(jnp.logical_not(short))(functools.partial(diag_tile, tk))

    _visit_tiles(last, slc_tile, sa_ref, sb_ref)

    o_slc = _normalized(acc_ref[...])

    nwc = (WINDOW + tq) // KCH
    c0 = jnp.maximum(qi * (tq // KCH) - WINDOW // KCH, 0)
    w0 = pl.multiple_of(c0 * KCH, KCH)
    kpos = w0 + lax.broadcasted_iota(jnp.int32, (WINDOW + tq, 1), 0)
    valid_w = (kpos <= tok_c) & (kpos > tok_c - WINDOW)
    vt_w = _with_ones_rows(jnp.concatenate([vwint_ref[c0 + c] for c in range(nwc)], axis=1))

    def win_softmax(c, s):
        s = jnp.where(valid_w, s, NEG)
        return jnp.exp2(s - jnp.max(s, axis=0, keepdims=True)).astype(MXU_DTYPE)

    o_win = jnp.concatenate(_pipelined_chains(
        n_chain, lambda c: _dot(kwin_ref[pl.ds(w0, WINDOW + tq), :], qr[:, chain_cols[c]]), win_softmax,
        lambda c, p: _normalized(_dot(vt_w, p))), axis=1)

    gt = gt_ref[...]
    outs = []
    for h in range(NSA_HEADS):
        rows = slice((h // HPG) * HEAD_DIM, (h // HPG + 1) * HEAD_DIM)
        cols = slice(h * tq, (h + 1) * tq)
        outs.append(gt[3 * h:3 * h + 1] * o_cmp[rows, cols] + gt[3 * h + 1:3 * h + 2] * o_slc[rows, cols]
                    + gt[3 * h + 2:3 * h + 3] * o_win[rows, cols])
    o_ref[...] = jnp.concatenate(outs, axis=0).T.astype(o_ref.dtype)


def _nsa(bsz, seq, qat, qart, gt, kc, vct, kslc, vslct, kwin, vwint, ovt, onehot):
    n = bsz * seq
    nq = seq // NSA_TQ
    nc = kc.shape[1]
    n_chain = NSA_HEADS * NSA_TQ // CHAIN
    qspec = lambda r: pl.BlockSpec((r, NSA_TQ), lambda b, i: (0, b * nq + i))
    kspec = pl.BlockSpec((seq, LANE), lambda b, i: (b, 0))
    vspec = pl.BlockSpec((seq // KCH, LANE, KCH), lambda b, i: (b, 0, 0))
    return pl.pallas_call(
        _nsa_kernel,
        grid=(bsz, nq),
        in_specs=[qspec(512), qspec(512), qspec(G_ROWS),
                  pl.BlockSpec((None, nc, LANE), lambda b, i: (b, 0, 0)),
                  pl.BlockSpec((None, LANE, nc), lambda b, i: (b, 0, 0)),
                  kspec, vspec, kspec, vspec, _const_spec(ovt.shape), _const_spec(onehot.shape)],
        out_specs=pl.BlockSpec((NSA_TQ, 512), lambda b, i: (b * nq + i, 0)),
        out_shape=jax.ShapeDtypeStruct((n, 512), MXU_DTYPE),
        scratch_shapes=[pltpu.VMEM((1, NSA_HEADS * NSA_TQ), jnp.float32),
                        pltpu.VMEM((LANE + ONES_ROWS, NSA_HEADS * NSA_TQ), jnp.float32),
                        pltpu.VMEM((n_chain, NSA_TK, CHAIN), jnp.float32),
                        pltpu.VMEM((n_chain, NSA_TK, CHAIN), jnp.float32),
                        pltpu.VMEM((LANE, NSA_HEADS * NSA_TQ), jnp.float32),
                        pltpu.VMEM((NSP, NSA_GROUPS * NSA_TQ), MXU_DTYPE)],
        compiler_params=_params("parallel", "parallel"),
        name="nsa_attention",
    )(qat, qart, gt, kc, vct, kslc, vslct, kwin, vwint, ovt, onehot)


def _diff_kernel(lambda_init, q_ref, k_ref, vt_ref, lq1_ref, lk1_ref, lq2_ref, lk2_ref, subln_ref,
                 o_ref, m_ref, acc_ref, sa_ref, sb_ref):
    qi = pl.program_id(2)
    tq, tk = DIFF_TQ, DIFF_TK
    cpt = tk // KCH
    t0 = qi * tq
    z = jnp.zeros((HEAD_DIM, tq), q_ref.dtype)
    qp = []
    for hl in range(DIFF_HPS):
        qt = q_ref[hl * DIFF_V_DIM:(hl + 1) * DIFF_V_DIM, :]
        qp.append(jnp.concatenate([jnp.concatenate([qt[:HEAD_DIM], z], axis=0),
                                   jnp.concatenate([z, qt[HEAD_DIM:]], axis=0)], axis=1))
    tok = t0 + lax.broadcasted_iota(jnp.int32, (1, CHAIN), 1)

    m_ref[...] = jnp.full(m_ref.shape, NEG, jnp.float32)
    acc_ref[...] = jnp.zeros(acc_ref.shape, jnp.float32)

    cph = 2 * tq // CHAIN
    n_chain = DIFF_HPS * cph

    def qk(j, c):
        hl, cc = divmod(c, cph)
        k = k_ref[pl.ds(pl.multiple_of(j * tk, tk), tk), hl * LANE:(hl + 1) * LANE]
        return _dot(k, qp[hl][:, cc * CHAIN:(cc + 1) * CHAIN])

    def tile(j, cur_ref, nxt_ref, masked):
        vt_all = jnp.concatenate([vt_ref[j * cpt + c] for c in range(cpt)], axis=1)
        vts = [_with_ones_rows(vt_all[hl * DIFF_V_DIM:(hl + 1) * DIFF_V_DIM]) for hl in range(DIFF_HPS)]
        kpos = j * tk + lax.broadcasted_iota(jnp.int32, (tk, 1), 0)
        m_all, acc_all = m_ref[...], acc_ref[...]
        m_out, acc_out = [], []
        pending = None
        for c in range(n_chain):
            cols = slice(c * CHAIN, (c + 1) * CHAIN)
            if nxt_ref is not None:
                nxt_ref[c] = qk(jnp.maximum(j - 1, 0), c)
            s = cur_ref[c]
            if masked:
                s = jnp.where(kpos <= tok + (c * CHAIN) % tq, s, NEG)
            m_old = m_all[:, cols]
            m_new = jnp.maximum(m_old, jnp.max(s, axis=0, keepdims=True))
            alpha = jnp.exp2(m_old - m_new)
            p = jnp.exp2(s - m_new)
            m_out.append(m_new)
            if pending is not None:
                acc_out.append(pending[0] + _dot(pending[2], pending[1]))
            pending = (acc_all[:, cols] * alpha, p.astype(MXU_DTYPE), vts[c // cph])
        acc_out.append(pending[0] + _dot(pending[2], pending[1]))
        m_ref[...] = jnp.concatenate(m_out, axis=1)
        acc_ref[...] = jnp.concatenate(acc_out, axis=1)

    last = t0 // tk
    for c in range(n_chain):
        sa_ref[c] = qk(last, c)
    tile(last, sa_ref, sb_ref, True)

    _visit_tiles(last, tile, sa_ref, sb_ref)

    lam = (jnp.exp(jnp.sum(lq1_ref[...] * lk1_ref[...], axis=1, keepdims=True))
           - jnp.exp(jnp.sum(lq2_ref[...] * lk2_ref[...], axis=1, keepdims=True)) + lambda_init)
    o_all = _normalized(acc_ref[...])
    for hl in range(DIFF_HPS):
        o = o_all[:, hl * 2 * tq:hl * 2 * tq + tq] - lam * o_all[:, hl * 2 * tq + tq:(hl + 1) * 2 * tq]
        o = o * lax.rsqrt(jnp.mean(o * o, axis=0, keepdims=True) + NORM_EPS)
        o = o * subln_ref[...] * (1.0 - lambda_init)
        o_ref[:, hl * DIFF_V_DIM:(hl + 1) * DIFF_V_DIM] = o.T.astype(o_ref.dtype)


def _diff(bsz, seq, lambda_init, qbt, kb, vbt, lq1, lk1, lq2, lk2, subln):
    n = bsz * seq
    nq = seq // DIFF_TQ
    vec = _const_spec((1, HEAD_DIM))
    return pl.pallas_call(
        functools.partial(_diff_kernel, lambda_init),
        grid=(bsz, DIFF_HEADS // DIFF_HPS, nq),
        in_specs=[pl.BlockSpec((DIFF_HPS * DIFF_V_DIM, DIFF_TQ), lambda b, h, i: (h, b * nq + i)),
                  pl.BlockSpec((seq, DIFF_HPS * LANE), lambda b, h, i: (b, h)),
                  pl.BlockSpec((seq // KCH, DIFF_HPS * DIFF_V_DIM, KCH), lambda b, h, i: (b, h, 0)),
                  vec, vec, vec, vec, _const_spec((DIFF_V_DIM, 1))],
        out_specs=pl.BlockSpec((DIFF_TQ, DIFF_HPS * DIFF_V_DIM), lambda b, h, i: (b * nq + i, h)),
        out_shape=jax.ShapeDtypeStruct((n, DIFF_HEADS * DIFF_V_DIM), MXU_DTYPE),
        scratch_shapes=[pltpu.VMEM((1, DIFF_HPS * 2 * DIFF_TQ), jnp.float32),
                        pltpu.VMEM((DIFF_V_DIM + ONES_ROWS, DIFF_HPS * 2 * DIFF_TQ), jnp.float32),
                        pltpu.VMEM((DIFF_HPS * 2 * DIFF_TQ // CHAIN, DIFF_TK, CHAIN), jnp.float32),
                        pltpu.VMEM((DIFF_HPS * 2 * DIFF_TQ // CHAIN, DIFF_TK, CHAIN), jnp.float32)],
        compiler_params=_params("parallel", "parallel", "parallel"),
        name="diff_attention",
    )(qbt, kb, vbt, lq1, lk1, lq2, lk2, subln)


def _merge_kernel(x_ref, g_ref, wm_ref, bm_ref, oa_ref, ob_ref, oc_ref, wa_ref, wb_ref, wc_ref, wo_ref, o_ref):
    x = x_ref[...]
    d = x.shape[1]
    hb = _rmsnorm_rows(x, g_ref[...]).astype(MXU_DTYPE)
    mixed = None
    for i, (br_ref, w_ref) in enumerate(((oa_ref, wa_ref), (ob_ref, wb_ref), (oc_ref, wc_ref))):
        cols = slice(i * d, (i + 1) * d)
        gate = jax.nn.sigmoid(_dot(hb, wm_ref[:, cols]) + bm_ref[:, cols])
        term = gate * _dot(br_ref[...], w_ref[...])
        mixed = term if mixed is None else mixed + term
    o_ref[...] = x + _dot(mixed.astype(MXU_DTYPE), wo_ref[...])


def _merge(x2, g, wm, bm, oa, ob, oc, wa, wb, wc, wo):
    n, d = x2.shape
    row = lambda w: pl.BlockSpec((TM, w), lambda i: (i, 0))
    return pl.pallas_call(
        _merge_kernel,
        grid=(n // TM,),
        in_specs=[row(d), _const_spec(g.shape), _const_spec(wm.shape), _const_spec(bm.shape),
                  row(BRANCH), row(BRANCH), row(BRANCH),
                  _const_spec(wa.shape), _const_spec(wb.shape), _const_spec(wc.shape), _const_spec(wo.shape)],
        out_specs=row(d),
        out_shape=jax.ShapeDtypeStruct((n, d), jnp.float32),
        compiler_params=_params("parallel"),
        name="merge_out",
    )(x2, g, wm, bm, oa, ob, oc, wa, wb, wc, wo)


def _ffn_kernel(final, n_chunks, x_ref, g_ref, w1_ref, w3_ref, w2_ref, gf_ref, o_ref):
    x = x_ref[...]
    hb = _rmsnorm_rows(x, g_ref[...]).astype(MXU_DTYPE)
    cw = w1_ref.shape[1] // n_chunks
    y = x
    for c in range(n_chunks):
        cols = slice(c * cw, (c + 1) * cw)
        act = jax.nn.silu(_dot(hb, w1_ref[:, cols])) * _dot(hb, w3_ref[:, cols])
        y = y + _dot(act.astype(MXU_DTYPE), w2_ref[cols, :])
    if final:
        y = _rmsnorm_rows(y, gf_ref[...])
    o_ref[...] = y


def _ffn(x2, g, w1, w3, w2, gf, final):
    n, d = x2.shape
    dff = w1.shape[1]
    n_chunks = 2 if dff % (2 * LANE) == 0 else 1
    row = pl.BlockSpec((TM, d), lambda i: (i, 0))
    return pl.pallas_call(
        functools.partial(_ffn_kernel, final, n_chunks),
        grid=(n // TM,),
        in_specs=[row, _const_spec(g.shape), _const_spec(w1.shape), _const_spec(w3.shape),
                  _const_spec(w2.shape), _const_spec(gf.shape)],
        out_specs=row,
        out_shape=jax.ShapeDtypeStruct((n, d), jnp.float32),
        compiler_params=_params("parallel"),
        name="ffn",
    )(x2, g, w1, w3, w2, gf)


def _overlap_t(seq):
    nc = seq // CMP_STRIDE
    c_start = jnp.arange(nc)[None, :] * CMP_STRIDE
    s_start = jnp.arange(NSP)[:, None] * SLC_BLOCK
    ov = (c_start < s_start + SLC_BLOCK) & (c_start + CMP_BLOCK > s_start)
    return ov.astype(MXU_DTYPE)


KV_WIDTH = NSA_GROUPS * HEAD_DIM
IN_SPLITS = (("q_a", NSA_HEADS * HEAD_DIM), ("k_cmp", KV_WIDTH), ("v_cmp", KV_WIDTH), ("k_slc", KV_WIDTH),
             ("v_slc", KV_WIDTH), ("k_win", KV_WIDTH), ("v_win", KV_WIDTH), ("g_a", 3 * NSA_HEADS),
             ("q_b", 2 * DIFF_HEADS * HEAD_DIM), ("k_b", 2 * DIFF_HEADS * HEAD_DIM),
             ("v_b", DIFF_HEADS * DIFF_V_DIM), ("uv_c", 2 * SGU_WIDTH))


def _split_in_proj(w):
    cols, off = {}, 0
    for name, width in IN_SPLITS:
        cols[name] = w[:, off:off + width]
        off += width
    assert off == w.shape[1]
    return cols


def _block_onehot(seq):
    return (jnp.arange(seq)[:, None] // SLC_BLOCK == jnp.arange(NSP)[None, :]).astype(MXU_DTYPE)


def kernel(x, positions, attn_norm, w_in, cmp_pos_k, cmp_k_w1, cmp_k_w2, cmp_pos_v, cmp_v_w1, cmp_v_w2, diff_lq1, diff_lk1, diff_lq2, diff_lk2, diff_subln, sgu_norm, sgu_w, sgu_b, w_branch_a, w_branch_b, w_branch_c, w_merge, b_merge, w_out, ffn_norm, w_ffn1, w_ffn3, w_ffn2, final_norm):
    bsz, seq, d = x.shape
    depth = w_in.shape[0]
    n = bsz * seq
    assert seq % NSA_TK == 0 and seq >= WINDOW + NSA_TQ and seq // SLC_BLOCK <= NSP and n % TM == 0
    bf = MXU_DTYPE

    tables = _rope_tables(positions)
    ovt = _overlap_t(seq)
    onehot = _block_onehot(seq)
    causal = jnp.tril(jnp.ones((SGU_CHUNK, SGU_CHUNK), dtype=bool))
    x2 = x.reshape(n, d)
    row = lambda v: v.reshape(1, -1)

    for l in range(depth):
        lambda_init = 0.8 - 0.6 * math.exp(-0.3 * l)
        w = _split_in_proj(w_in[l])
        wp = jnp.concatenate([w["k_cmp"], w["v_cmp"], w["k_slc"], w["k_win"], w["k_b"], w["uv_c"]],
                             axis=1).astype(bf)
        wt = jnp.concatenate([w["q_a"], w["v_slc"], w["v_win"], w["g_a"],
                              jnp.zeros((d, G_ROWS - 3 * NSA_HEADS), x.dtype), w["q_b"], w["v_b"]],
                             axis=1).T.astype(bf)
        sgu_wm = jnp.where(causal[None], sgu_w[l], 0.0).astype(bf)
        sgu_bias = jnp.repeat(sgu_b[l].T, SGU_WIDTH // SGU_GROUPS, axis=1)

        (kcmp, vcmp, kslc, kwin, kb, oc, qat, qart, vslct, vwint, gt, qbt, vbt) = _inproj(
            x2, row(attn_norm[l]), wp, wt, tables, row(sgu_norm[l]), sgu_wm, sgu_bias)

        kc = _compress(kcmp, bsz, cmp_pos_k[l], cmp_k_w1[l], cmp_k_w2[l], transposed=False)
        vct = _compress(vcmp, bsz, cmp_pos_v[l], cmp_v_w1[l], cmp_v_w2[l], transposed=True)

        oa = _nsa(bsz, seq, qat, qart, gt, kc, vct, kslc, vslct, kwin, vwint, ovt, onehot)
        ob = _diff(bsz, seq, lambda_init, qbt, kb, vbt, row(diff_lq1[l]), row(diff_lk1[l]),
                   row(diff_lq2[l]), row(diff_lk2[l]), diff_subln[l].reshape(DIFF_V_DIM, 1))

        x2 = _merge(x2, row(attn_norm[l]), w_merge[l].astype(bf), row(b_merge[l]), oa, ob, oc,
                    w_branch_a[l].astype(bf), w_branch_b[l].astype(bf), w_branch_c[l].astype(bf),
                    w_out[l].astype(bf))
        x2 = _ffn(x2, row(ffn_norm[l]), w_ffn1[l].astype(bf), w_ffn3[l].astype(bf), w_ffn2[l].astype(bf),
                  row(final_norm), final=(l == depth - 1))
    return x2.reshape(bsz, seq, d)
```

```python
import functools
import math

import jax
import jax.numpy as jnp
from jax import lax
from jax.experimental import pallas as pl
from jax.experimental.pallas import tpu as pltpu

HEAD_DIM = 64
HALF = HEAD_DIM // 2
ROPE_THETA = 10000.0
NORM_EPS = 1e-6
BIG = 1e9
NEG = -1e30
REMOVED = -3e38

NSA_HEADS = 8
NSA_GROUPS = 2
HPG = NSA_HEADS // NSA_GROUPS
CMP_BLOCK = 32
CMP_STRIDE = 16
CMP_HIDDEN = 256
SLC_BLOCK = 64
SLC_TOPK = 16
WINDOW = 512
DIFF_HEADS = 4
DIFF_V_DIM = 2 * HEAD_DIM
SGU_CHUNK = 128
SGU_GROUPS = 4
SGU_WIDTH = 512
BRANCH = 512
SCALE = HEAD_DIM ** -0.5
QSCALE = SCALE * math.log2(math.e)
ONES_ROWS = 16

LANE = 128
NSP = 128
MXU_DTYPE = jnp.bfloat16
VMEM_LIMIT = 56 * 1024 * 1024

TM = 512
NSA_TQ = 256
NSA_TK = 512
DIFF_TQ = 512
DIFF_TK = 512
DIFF_HPS = 2
CHAIN = 256
SWEEP_UNROLL = 4
CAUSAL_VARIANTS = 4
KCH = 128


def _params(*sem):
    return pltpu.CompilerParams(dimension_semantics=sem, vmem_limit_bytes=VMEM_LIMIT)


def _const_spec(shape):
    nd = len(shape)
    return pl.BlockSpec(shape, lambda *_: (0,) * nd, pipeline_mode=pl.Buffered(1))


def _dot(a, b):
    return jnp.dot(a, b, preferred_element_type=jnp.float32)


def _dot_nt(a, b):
    return lax.dot_general(a, b, (((1,), (1,)), ((), ())), preferred_element_type=jnp.float32)


def _dot_tn(a, b):
    return lax.dot_general(a, b, (((0,), (0,)), ((), ())), preferred_element_type=jnp.float32)


def _rmsnorm_rows(x, g):
    return x * lax.rsqrt(jnp.mean(x * x, axis=-1, keepdims=True) + NORM_EPS) * g


def _tables_kernel(pos_row_ref, f_col_ref, cos_ref, sin_ref, cost_ref, sint_ref):
    angt = f_col_ref[...] * pos_row_ref[...].astype(jnp.float32)
    cost, sint = jnp.cos(angt), jnp.sin(angt)
    cost_ref[...] = cost
    sint_ref[...] = sint
    reps = LANE // HALF
    cos_ref[...] = jnp.concatenate([cost] * reps, axis=0).T
    sin = jnp.concatenate([sint] * reps, axis=0).T
    lane = lax.broadcasted_iota(jnp.int32, sin.shape, 1)
    sin_ref[...] = jnp.where((lane % HEAD_DIM) < HALF, -sin, sin)


def _rope_tables(positions):
    n = positions.size
    inv_freq = ROPE_THETA ** (-jnp.arange(HALF, dtype=jnp.float32) / HALF)
    f_col = inv_freq.reshape(HALF, 1)
    pos_row = positions.reshape(1, n)
    return pl.pallas_call(
        _tables_kernel,
        grid=(n // TM,),
        in_specs=[pl.BlockSpec((1, TM), lambda i: (0, i)), _const_spec((HALF, 1))],
        out_specs=[pl.BlockSpec((TM, LANE), lambda i: (i, 0)),
                   pl.BlockSpec((TM, LANE), lambda i: (i, 0)),
                   pl.BlockSpec((HALF, TM), lambda i: (0, i)),
                   pl.BlockSpec((HALF, TM), lambda i: (0, i))],
        out_shape=[jax.ShapeDtypeStruct((n, LANE), jnp.float32),
                   jax.ShapeDtypeStruct((n, LANE), jnp.float32),
                   jax.ShapeDtypeStruct((HALF, n), jnp.float32),
                   jax.ShapeDtypeStruct((HALF, n), jnp.float32)],
        compiler_params=_params("parallel"),
        name="rope_tables",
    )(pos_row, f_col)


P_KVC, P_KSLC, P_KWIN, P_KB, P_UV, P_END = 0, 256, 384, 512, 1024, 2048
T_QA, T_VSLC, T_VWIN, T_G, T_QB, T_VB, T_END = 0, 512, 640, 768, 800, 1312, 1824
G_ROWS = T_QB - T_G


def _rope_plain(y, cos, sin):
    lane = lax.broadcasted_iota(jnp.int32, y.shape, 1)
    swapped = jnp.where((lane % HEAD_DIM) < HALF, pltpu.roll(y, LANE - HALF, 1), pltpu.roll(y, HALF, 1))
    return y * cos + swapped * sin


def _rope_t(y, cos, sin):
    y1, y2 = y[:HALF], y[HALF:]
    return jnp.concatenate([y1 * cos - y2 * sin, y2 * cos + y1 * sin], axis=0)


def _inproj_kernel(x_ref, g_ref, wp_ref, wt_ref, cos_ref, sin_ref, cost_ref, sint_ref,
                   sgu_g_ref, sgu_w_ref, sgu_b_ref,
                   kcmp_ref, vcmp_ref, kslc_ref, kwin_ref, kb_ref, oc_ref,
                   qat_ref, qart_ref, vslct_ref, vwint_ref, gt_ref, qbt_ref, vbt_ref):
    x = x_ref[...]
    hb = _rmsnorm_rows(x, g_ref[...]).astype(MXU_DTYPE)
    cos, sin = cos_ref[...], sin_ref[...]
    cost, sint = cost_ref[...], sint_ref[...]
    tm = x.shape[0]

    z = jax.nn.gelu(_dot(hb, wp_ref[:, P_UV:P_END]))
    u = z[:, :SGU_WIDTH]
    vn = _rmsnorm_rows(z[:, SGU_WIDTH:], sgu_g_ref[...]).astype(MXU_DTYPE)

    kvc = _dot(hb, wp_ref[:, P_KVC:P_KSLC])
    kcmp_ref[...] = kvc[:, :LANE]
    vcmp_ref[...] = kvc[:, LANE:]
    kslc_ref[...] = _rope_plain(_dot(hb, wp_ref[:, P_KSLC:P_KWIN]), cos, sin).astype(kslc_ref.dtype)
    kwin_ref[...] = _rope_plain(_dot(hb, wp_ref[:, P_KWIN:P_KB]), cos, sin).astype(kwin_ref.dtype)
    kb = _dot(hb, wp_ref[:, P_KB:P_UV])
    for c in range((P_UV - P_KB) // LANE):
        sl = slice(c * LANE, (c + 1) * LANE)
        kb_ref[:, sl] = _rope_plain(kb[:, sl], cos, sin).astype(kb_ref.dtype)

    qat = _dot_nt(wt_ref[T_QA:T_VSLC, :], hb) * QSCALE
    qat_ref[...] = qat.astype(qat_ref.dtype)
    for h in range(NSA_HEADS):
        sl = slice(h * HEAD_DIM, (h + 1) * HEAD_DIM)
        qart_ref[sl, :] = _rope_t(qat[sl], cost, sint).astype(qart_ref.dtype)
    vst = _dot_nt(wt_ref[T_VSLC:T_VWIN, :], hb).astype(vslct_ref.dtype)
    vwt = _dot_nt(wt_ref[T_VWIN:T_G, :], hb).astype(vwint_ref.dtype)
    for c in range(tm // KCH):
        sl = slice(c * KCH, (c + 1) * KCH)
        vslct_ref[c] = vst[:, sl]
        vwint_ref[c] = vwt[:, sl]
    gt_ref[...] = jax.nn.sigmoid(_dot_nt(wt_ref[T_G:T_QB, :], hb))
    qbt = _dot_nt(wt_ref[T_QB:T_VB, :], hb) * QSCALE
    for h in range(2 * DIFF_HEADS):
        sl = slice(h * HEAD_DIM, (h + 1) * HEAD_DIM)
        qbt_ref[sl, :] = _rope_t(qbt[sl], cost, sint).astype(qbt_ref.dtype)
    vbt = _dot_nt(wt_ref[T_VB:T_END, :], hb).astype(vbt_ref.dtype)
    for c in range(tm // KCH):
        vbt_ref[c] = vbt[:, c * KCH:(c + 1) * KCH]

    gdim = SGU_WIDTH // SGU_GROUPS
    for ci in range(tm // SGU_CHUNK):
        rows = slice(ci * SGU_CHUNK, (ci + 1) * SGU_CHUNK)
        for gi in range(SGU_GROUPS):
            cols = slice(gi * gdim, (gi + 1) * gdim)
            s = _dot(sgu_w_ref[gi], vn[rows, cols]) + sgu_b_ref[:, cols]
            oc_ref[rows, cols] = (u[rows, cols] * s).astype(oc_ref.dtype)


def _inproj(x2, g, wp, wt, tables, sgu_g, sgu_w, sgu_b):
    n, d = x2.shape
    cos, sin, cost, sint = tables
    row = lambda w: pl.BlockSpec((TM, w), lambda i: (i, 0))
    colt = lambda r: pl.BlockSpec((r, TM), lambda i: (0, i))
    chunk = lambda r: pl.BlockSpec((TM // KCH, r, KCH), lambda i: (i, 0, 0))
    bf = MXU_DTYPE
    outs = [
        (row(LANE), jax.ShapeDtypeStruct((n, LANE), jnp.float32)),
        (row(LANE), jax.ShapeDtypeStruct((n, LANE), jnp.float32)),
        (row(LANE), jax.ShapeDtypeStruct((n, LANE), bf)),
        (row(LANE), jax.ShapeDtypeStruct((n, LANE), bf)),
        (row(512), jax.ShapeDtypeStruct((n, 512), bf)),
        (row(SGU_WIDTH), jax.ShapeDtypeStruct((n, SGU_WIDTH), bf)),
        (colt(512), jax.ShapeDtypeStruct((512, n), bf)),
        (colt(512), jax.ShapeDtypeStruct((512, n), bf)),
        (chunk(LANE), jax.ShapeDtypeStruct((n // KCH, LANE, KCH), bf)),
        (chunk(LANE), jax.ShapeDtypeStruct((n // KCH, LANE, KCH), bf)),
        (colt(G_ROWS), jax.ShapeDtypeStruct((G_ROWS, n), jnp.float32)),
        (colt(512), jax.ShapeDtypeStruct((512, n), bf)),
        (chunk(512), jax.ShapeDtypeStruct((n // KCH, 512, KCH), bf)),
    ]
    return pl.pallas_call(
        _inproj_kernel,
        grid=(n // TM,),
        in_specs=[row(d), _const_spec((1, d)), _const_spec(wp.shape), _const_spec(wt.shape),
                  row(LANE), row(LANE), colt(HALF), colt(HALF),
                  _const_spec(sgu_g.shape), _const_spec(sgu_w.shape), _const_spec(sgu_b.shape)],
        out_specs=[o[0] for o in outs],
        out_shape=[o[1] for o in outs],
        compiler_params=_params("parallel"),
        name="inproj",
    )(x2, g, wp, wt, cos, sin, cost, sint, sgu_g, sgu_w, sgu_b)


def _chunked_tokens(x_ref, g, nc):
    lanes = slice(g * HEAD_DIM, (g + 1) * HEAD_DIM)
    return jnp.concatenate([x_ref[pl.ds(t, nc, stride=CMP_STRIDE), :][:, lanes] for t in range(CMP_STRIDE)],
                           axis=1)


def _compress_k_kernel(x_ref, pa_ref, pb_ref, w1a_ref, w1b_ref, w2_ref, o_ref):
    nc = o_ref.shape[0]
    acc = jnp.zeros(o_ref.shape, jnp.float32)
    for g in range(NSA_GROUPS):
        xg = _chunked_tokens(x_ref, g, nc)
        a = _dot((xg + pa_ref[...]).astype(MXU_DTYPE), w1a_ref[...])
        b = _dot((xg + pb_ref[...]).astype(MXU_DTYPE), w1b_ref[...])
        hid = jax.nn.gelu(a + pltpu.roll(b, nc - 1, 0))
        acc = acc + _dot(hid.astype(MXU_DTYPE), w2_ref[g])
    o_ref[...] = acc.astype(o_ref.dtype)


def _compress_v_kernel(x_ref, pa_ref, pb_ref, w1at_ref, w1bt_ref, w2t_ref, o_ref):
    nc = o_ref.shape[1]
    acc = jnp.zeros(o_ref.shape, jnp.float32)
    for g in range(NSA_GROUPS):
        xg = _chunked_tokens(x_ref, g, nc)
        a = _dot_nt(w1at_ref[...], (xg + pa_ref[...]).astype(MXU_DTYPE))
        b = _dot_nt(w1bt_ref[...], (xg + pb_ref[...]).astype(MXU_DTYPE))
        hid = jax.nn.gelu(a + pltpu.roll(b, nc - 1, 1))
        acc = acc + _dot(w2t_ref[g], hid.astype(MXU_DTYPE))
    o_ref[...] = acc.astype(o_ref.dtype)


def _compress(kvc, b, pos, w1, w2, transposed):
    seq = kvc.shape[0] // b
    nc, cw = seq // CMP_STRIDE, CMP_STRIDE * HEAD_DIM
    pos_flat = pos.reshape(1, CMP_BLOCK * HEAD_DIM)
    pa, pb = pos_flat[:, :cw], pos_flat[:, cw:]
    w1a, w1b = w1[:cw].astype(MXU_DTYPE), w1[cw:].astype(MXU_DTYPE)
    w2p = jnp.zeros((NSA_GROUPS, CMP_HIDDEN, LANE), jnp.float32)
    for g in range(NSA_GROUPS):
        w2p = w2p.at[g, :, g * HEAD_DIM:(g + 1) * HEAD_DIM].set(w2)
    w2p = w2p.astype(MXU_DTYPE)
    x_spec = pl.BlockSpec((seq, kvc.shape[1]), lambda i: (i, 0))
    if not transposed:
        return pl.pallas_call(
            _compress_k_kernel, grid=(b,),
            in_specs=[x_spec, _const_spec(pa.shape), _const_spec(pb.shape), _const_spec(w1a.shape),
                      _const_spec(w1b.shape), _const_spec(w2p.shape)],
            out_specs=pl.BlockSpec((None, nc, LANE), lambda i: (i, 0, 0)),
            out_shape=jax.ShapeDtypeStruct((b, nc, LANE), MXU_DTYPE),
            compiler_params=_params("parallel"), name="compress_k",
        )(kvc, pa, pb, w1a, w1b, w2p)
    w1at, w1bt, w2pt = w1a.T, w1b.T, jnp.swapaxes(w2p, 1, 2)
    return pl.pallas_call(
        _compress_v_kernel, grid=(b,),
        in_specs=[x_spec, _const_spec(pa.shape), _const_spec(pb.shape), _const_spec(w1at.shape),
                  _const_spec(w1bt.shape), _const_spec(w2pt.shape)],
        out_specs=pl.BlockSpec((None, LANE, nc), lambda i: (i, 0, 0)),
        out_shape=jax.ShapeDtypeStruct((b, LANE, nc), MXU_DTYPE),
        compiler_params=_params("parallel"), name="compress_v",
    )(kvc, pa, pb, w1at, w1bt, w2pt)


def _group_queries_t(qt):
    z = jnp.zeros((HEAD_DIM, qt.shape[1]), qt.dtype)
    cols = []
    for h in range(NSA_HEADS):
        slab = qt[h * HEAD_DIM:(h + 1) * HEAD_DIM]
        cols.append(jnp.concatenate([slab, z] if h < HPG else [z, slab], axis=0))
    return jnp.concatenate(cols, axis=1)


def _with_ones_rows(vt):
    return jnp.concatenate([vt, jnp.ones((ONES_ROWS, vt.shape[1]), vt.dtype)], axis=0)


def _normalized(acc):
    dims = acc.shape[0] - ONES_ROWS
    return acc[:dims] * (1.0 / acc[dims:dims + 1])


def _visit_tiles(last, tile, sa_ref, sb_ref):
    def slots(u):
        return (sb_ref, sa_ref) if u % 2 == 0 else (sa_ref, sb_ref)

    def group(i, carry):
        j = last - 1 - SWEEP_UNROLL * i
        for u in range(SWEEP_UNROLL):
            tile(j - u, *slots(u), False)
        return carry

    lax.fori_loop(0, last // SWEEP_UNROLL, group, 0)
    rem = last % SWEEP_UNROLL
    for u in range(SWEEP_UNROLL - 1):
        cur_ref, nxt_ref = slots(u)
        pl.when(rem == u + 1)(functools.partial(tile, 0, cur_ref, None, False))
        if u < SWEEP_UNROLL - 2:
            pl.when(rem > u + 1)(functools.partial(tile, rem - 1 - u, cur_ref, nxt_ref, False))


def _pipelined_chains(n_chain, qk, softmax, pv):
    outs = []
    s_next = qk(0)
    pending = None
    for c in range(n_chain):
        s = s_next
        if c + 1 < n_chain:
            s_next = qk(c + 1)
        r = softmax(c, s)
        if pending is not None:
            outs.append(pv(*pending))
        pending = (c, r)
    outs.append(pv(*pending))
    return outs


def _nsa_kernel(qat_ref, qart_ref, gt_ref, kc_ref, vct_ref, kslc_ref, vslct_ref, kwin_ref, vwint_ref,
                ovt_ref, e_ref, o_ref, m_ref, acc_ref, sa_ref, sb_ref, ocmp_ref, bias_ref):
    qi = pl.program_id(1)
    tq, tk = NSA_TQ, NSA_TK
    nc = kc_ref.shape[0]
    n_chain = NSA_HEADS * tq // CHAIN
    hpc = CHAIN // tq
    t0 = qi * tq
    tok = t0 + lax.broadcasted_iota(jnp.int32, (1, tq), 1)
    tok_c = jnp.concatenate([tok] * hpc, axis=1)
    qc = _group_queries_t(qat_ref[...])
    qr = _group_queries_t(qart_ref[...])
    chain_cols = [slice(c * CHAIN, (c + 1) * CHAIN) for c in range(n_chain)]
    chain_group = [(c * hpc) // HPG for c in range(n_chain)]

    seq = kslc_ref.shape[0]
    tok2 = jnp.concatenate([tok] * NSA_GROUPS, axis=1)
    has_key = tok_c >= CMP_BLOCK - 1
    variant = (CAUSAL_VARIANTS * (t0 + tq) + seq - 1) // seq - 1

    def cmp_topk(rows_c, rows_b, forced_distinct):
        cmp_end = lax.broadcasted_iota(jnp.int32, (rows_c, 1), 0) * CMP_STRIDE + (CMP_BLOCK - 1)
        mask_c = cmp_end <= tok_c
        vo = jnp.concatenate([_with_ones_rows(vct_ref[:, :rows_c]), ovt_ref[:rows_b, :rows_c]], axis=0)
        imp_g = [None] * NSA_GROUPS

        def cmp_softmax(c, s):
            s = jnp.where(mask_c, s, NEG)
            return jnp.exp2(s - jnp.max(s, axis=0, keepdims=True)).astype(MXU_DTYPE)

        def cmp_pv(c, p):
            r = _dot(vo, p)
            inv = jnp.where(has_key, 1.0 / r[LANE:LANE + 1], 0.0)
            imp_c = r[LANE + ONES_ROWS:] * inv
            for i in range(hpc):
                g = chain_group[c]
                part = imp_c[:, i * tq:(i + 1) * tq]
                imp_g[g] = part if imp_g[g] is None else imp_g[g] + part
            return r[:LANE] * inv

        ocmp_ref[...] = jnp.concatenate(_pipelined_chains(
            n_chain, lambda c: _dot(kc_ref[:rows_c, :], qc[:, chain_cols[c]]), cmp_softmax, cmp_pv), axis=1)
        imp = jnp.concatenate(imp_g, axis=1)

        blk = lax.broadcasted_iota(jnp.int32, (rows_b, 1), 0)
        blk_f = blk.astype(jnp.float32)
        cur = tok2 // SLC_BLOCK
        forced = (blk == 0) | (blk == cur) | (blk == cur - 1)
        score = jnp.where(blk * SLC_BLOCK <= tok2, imp, -BIG)
        if forced_distinct:
            rounds = SLC_TOPK - 3
            bias = jnp.where(forced, 0.0, NEG)
            score = jnp.where(forced, REMOVED, score)
        else:
            rounds = SLC_TOPK
            bias = jnp.full(score.shape, NEG, jnp.float32)
            score = jnp.where(forced, BIG, score)
        for _ in range(rounds):
            best = jnp.max(score, axis=0, keepdims=True)
            first = jnp.min(jnp.where(score == best, blk_f, float(NSP)), axis=0, keepdims=True)
            pick = blk_f == first
            bias = jnp.where(pick, 0.0, bias)
            score = jnp.where(pick, REMOVED, score)
        bias_ref[:rows_b, :] = bias.astype(MXU_DTYPE)
        if rows_b < NSP:
            bias_ref[rows_b:, :] = jnp.full((NSP - rows_b, NSA_GROUPS * tq), NEG, MXU_DTYPE)

    for v in range(CAUSAL_VARIANTS):
        rows_c = nc * (v + 1) // CAUSAL_VARIANTS
        rows_b = min(NSP, -(-(seq // SLC_BLOCK * (v + 1) // CAUSAL_VARIANTS) // 32) * 32)
        min_t0 = seq * v // CAUSAL_VARIANTS - tq + 1
        pl.when(variant == v)(functools.partial(cmp_topk, rows_c, rows_b, v > 0 and min_t0 >= 2 * SLC_BLOCK))
    o_cmp = ocmp_ref[...]
    bias = bias_ref[...]

    qa = []
    for c in range(n_chain):
        g = chain_group[c]
        bias_c = jnp.concatenate([bias[:, g * tq:(g + 1) * tq]] * hpc, axis=1)
        qa.append(jnp.concatenate([qr[:, chain_cols[c]], bias_c], axis=0))
    m_ref[...] = jnp.full(m_ref.shape, NEG, jnp.float32)
    acc_ref[...] = jnp.zeros(acc_ref.shape, jnp.float32)
    cpt = tk // KCH

    def slc_qk(j, c, nk=tk):
        k0 = pl.multiple_of(j * tk, tk)
        ka = jnp.concatenate([kslc_ref[pl.ds(k0, nk), :], e_ref[pl.ds(k0, nk), :]], axis=1)
        return _dot(ka, qa[c])

    def slc_tile(j, cur_ref, nxt_ref, masked, nk=tk):
        vt = _with_ones_rows(jnp.concatenate([vslct_ref[j * cpt + c] for c in range(nk // KCH)], axis=1))
        kpos = j * tk + lax.broadcasted_iota(jnp.int32, (nk, 1), 0)
        m_all, acc_all = m_ref[...], acc_ref[...]
        m_out, acc_out = [], []
        pending = None
        for c in range(n_chain):
            cols = chain_cols[c]
            if nxt_ref is not None:
                nxt_ref[c] = slc_qk(jnp.maximum(j - 1, 0), c)
            s = cur_ref[c, :nk, :]
            if masked:
                s = jnp.where(kpos <= tok_c, s, NEG)
            m_old = m_all[:, cols]
            m_new = jnp.maximum(m_old, jnp.max(s, axis=0, keepdims=True))
            alpha = jnp.exp2(m_old - m_new)
            p = jnp.exp2(s - m_new)
            m_out.append(m_new)
            if pending is not None:
                acc_out.append(pending[0] + _dot(vt, pending[1]))
            pending = (acc_all[:, cols] * alpha, p.astype(MXU_DTYPE))
        acc_out.append(pending[0] + _dot(vt, pending[1]))
        m_ref[...] = jnp.concatenate(m_out, axis=1)
        acc_ref[...] = jnp.concatenate(acc_out, axis=1)

    last = t0 // tk

    def diag_tile(nk):
        for c in range(n_chain):
            sa_ref[c, :nk, :] = slc_qk(last, c, nk)
        slc_tile(last, sa_ref, sb_ref, True, nk)

    short = t0 - last * tk + tq <= tk // 2
    pl.when(short)(functools.partial(diag_tile, tk // 2))
    pl.when(jnp.logical_not(short))(functools.partial(diag_tile, tk))

    _visit_tiles(last, slc_tile, sa_ref, sb_ref)

    o_slc = _normalized(acc_ref[...])

    nwc = (WINDOW + tq) // KCH
    c0 = jnp.maximum(qi * (tq // KCH) - WINDOW // KCH, 0)
    w0 = pl.multiple_of(c0 * KCH, KCH)
    kpos = w0 + lax.broadcasted_iota(jnp.int32, (WINDOW + tq, 1), 0)
    valid_w = (kpos <= tok_c) & (kpos > tok_c - WINDOW)
    vt_w = _with_ones_rows(jnp.concatenate([vwint_ref[c0 + c] for c in range(nwc)], axis=1))

    def win_softmax(c, s):
        s = jnp.where(valid_w, s, NEG)
        return jnp.exp2(s - jnp.max(s, axis=0, keepdims=True)).astype(MXU_DTYPE)

    o_win = jnp.concatenate(_pipelined_chains(
        n_chain, lambda c: _dot(kwin_ref[pl.ds(w0, WINDOW + tq), :], qr[:, chain_cols[c]]), win_softmax,
        lambda c, p: _normalized(_dot(vt_w, p))), axis=1)

    gt = gt_ref[...]
    outs = []
    for h in range(NSA_HEADS):
        rows = slice((h // HPG) * HEAD_DIM, (h // HPG + 1) * HEAD_DIM)
        cols = slice(h * tq, (h + 1) * tq)
        outs.append(gt[3 * h:3 * h + 1] * o_cmp[rows, cols] + gt[3 * h + 1:3 * h + 2] * o_slc[rows, cols]
                    + gt[3 * h + 2:3 * h + 3] * o_win[rows, cols])
    o_ref[...] = jnp.concatenate(outs, axis=0).astype(o_ref.dtype)


def _nsa(bsz, seq, qat, qart, gt, kc, vct, kslc, vslct, kwin, vwint, ovt, onehot):
    n = bsz * seq
    nq = seq // NSA_TQ
    nc = kc.shape[1]
    n_chain = NSA_HEADS * NSA_TQ // CHAIN
    qspec = lambda r: pl.BlockSpec((r, NSA_TQ), lambda b, i: (0, b * nq + i))
    kspec = pl.BlockSpec((seq, LANE), lambda b, i: (b, 0))
    vspec = pl.BlockSpec((seq // KCH, LANE, KCH), lambda b, i: (b, 0, 0))
    return pl.pallas_call(
        _nsa_kernel,
        grid=(bsz, nq),
        in_specs=[qspec(512), qspec(512), qspec(G_ROWS),
                  pl.BlockSpec((None, nc, LANE), lambda b, i: (b, 0, 0)),
                  pl.BlockSpec((None, LANE, nc), lambda b, i: (b, 0, 0)),
                  kspec, vspec, kspec, vspec, _const_spec(ovt.shape), _const_spec(onehot.shape)],
        out_specs=pl.BlockSpec((512, NSA_TQ), lambda b, i: (0, b * nq + i)),
        out_shape=jax.ShapeDtypeStruct((512, n), MXU_DTYPE),
        scratch_shapes=[pltpu.VMEM((1, NSA_HEADS * NSA_TQ), jnp.float32),
                        pltpu.VMEM((LANE + ONES_ROWS, NSA_HEADS * NSA_TQ), jnp.float32),
                        pltpu.VMEM((n_chain, NSA_TK, CHAIN), jnp.float32),
                        pltpu.VMEM((n_chain, NSA_TK, CHAIN), jnp.float32),
                        pltpu.VMEM((LANE, NSA_HEADS * NSA_TQ), jnp.float32),
                        pltpu.VMEM((NSP, NSA_GROUPS * NSA_TQ), MXU_DTYPE)],
        compiler_params=_params("parallel", "parallel"),
        name="nsa_attention",
    )(qat, qart, gt, kc, vct, kslc, vslct, kwin, vwint, ovt, onehot)


def _diff_kernel(lambda_init, q_ref, k_ref, vt_ref, lq1_ref, lk1_ref, lq2_ref, lk2_ref, subln_ref,
                 o_ref, m_ref, acc_ref, sa_ref, sb_ref):
    qi = pl.program_id(2)
    tq, tk = DIFF_TQ, DIFF_TK
    cpt = tk // KCH
    t0 = qi * tq
    z = jnp.zeros((HEAD_DIM, tq), q_ref.dtype)
    qp = []
    for hl in range(DIFF_HPS):
        qt = q_ref[hl * DIFF_V_DIM:(hl + 1) * DIFF_V_DIM, :]
        qp.append(jnp.concatenate([jnp.concatenate([qt[:HEAD_DIM], z], axis=0),
                                   jnp.concatenate([z, qt[HEAD_DIM:]], axis=0)], axis=1))
    tok = t0 + lax.broadcasted_iota(jnp.int32, (1, CHAIN), 1)

    m_ref[...] = jnp.full(m_ref.shape, NEG, jnp.float32)
    acc_ref[...] = jnp.zeros(acc_ref.shape, jnp.float32)

    cph = 2 * tq // CHAIN
    n_chain = DIFF_HPS * cph

    def qk(j, c):
        hl, cc = divmod(c, cph)
        k = k_ref[pl.ds(pl.multiple_of(j * tk, tk), tk), hl * LANE:(hl + 1) * LANE]
        return _dot(k, qp[hl][:, cc * CHAIN:(cc + 1) * CHAIN])

    def tile(j, cur_ref, nxt_ref, masked):
        vt_all = jnp.concatenate([vt_ref[j * cpt + c] for c in range(cpt)], axis=1)
        vts = [_with_ones_rows(vt_all[hl * DIFF_V_DIM:(hl + 1) * DIFF_V_DIM]) for hl in range(DIFF_HPS)]
        kpos = j * tk + lax.broadcasted_iota(jnp.int32, (tk, 1), 0)
        m_all, acc_all = m_ref[...], acc_ref[...]
        m_out, acc_out = [], []
        pending = None
        for c in range(n_chain):
            cols = slice(c * CHAIN, (c + 1) * CHAIN)
            if nxt_ref is not None:
                nxt_ref[c] = qk(jnp.maximum(j - 1, 0), c)
            s = cur_ref[c]
            if masked:
                s = jnp.where(kpos <= tok + (c * CHAIN) % tq, s, NEG)
            m_old = m_all[:, cols]
            m_new = jnp.maximum(m_old, jnp.max(s, axis=0, keepdims=True))
            alpha = jnp.exp2(m_old - m_new)
            p = jnp.exp2(s - m_new)
            m_out.append(m_new)
            if pending is not None:
                acc_out.append(pending[0] + _dot(pending[2], pending[1]))
            pending = (acc_all[:, cols] * alpha, p.astype(MXU_DTYPE), vts[c // cph])
        acc_out.append(pending[0] + _dot(pending[2], pending[1]))
        m_ref[...] = jnp.concatenate(m_out, axis=1)
        acc_ref[...] = jnp.concatenate(acc_out, axis=1)

    last = t0 // tk
    for c in range(n_chain):
        sa_ref[c] = qk(last, c)
    tile(last, sa_ref, sb_ref, True)

    _visit_tiles(last, tile, sa_ref, sb_ref)

    lam = (jnp.exp(jnp.sum(lq1_ref[...] * lk1_ref[...], axis=1, keepdims=True))
           - jnp.exp(jnp.sum(lq2_ref[...] * lk2_ref[...], axis=1, keepdims=True)) + lambda_init)
    o_all = _normalized(acc_ref[...])
    for hl in range(DIFF_HPS):
        o = o_all[:, hl * 2 * tq:hl * 2 * tq + tq] - lam * o_all[:, hl * 2 * tq + tq:(hl + 1) * 2 * tq]
        o = o * lax.rsqrt(jnp.mean(o * o, axis=0, keepdims=True) + NORM_EPS)
        o = o * subln_ref[...] * (1.0 - lambda_init)
        o_ref[hl * DIFF_V_DIM:(hl + 1) * DIFF_V_DIM, :] = o.astype(o_ref.dtype)


def _diff(bsz, seq, lambda_init, qbt, kb, vbt, lq1, lk1, lq2, lk2, subln):
    n = bsz * seq
    nq = seq // DIFF_TQ
    vec = _const_spec((1, HEAD_DIM))
    return pl.pallas_call(
        functools.partial(_diff_kernel, lambda_init),
        grid=(bsz, DIFF_HEADS // DIFF_HPS, nq),
        in_specs=[pl.BlockSpec((DIFF_HPS * DIFF_V_DIM, DIFF_TQ), lambda b, h, i: (h, b * nq + i)),
                  pl.BlockSpec((seq, DIFF_HPS * LANE), lambda b, h, i: (b, h)),
                  pl.BlockSpec((seq // KCH, DIFF_HPS * DIFF_V_DIM, KCH), lambda b, h, i: (b, h, 0)),
                  vec, vec, vec, vec, _const_spec((DIFF_V_DIM, 1))],
        out_specs=pl.BlockSpec((DIFF_HPS * DIFF_V_DIM, DIFF_TQ), lambda b, h, i: (h, b * nq + i)),
        out_shape=jax.ShapeDtypeStruct((DIFF_HEADS * DIFF_V_DIM, n), MXU_DTYPE),
        scratch_shapes=[pltpu.VMEM((1, DIFF_HPS * 2 * DIFF_TQ), jnp.float32),
                        pltpu.VMEM((DIFF_V_DIM + ONES_ROWS, DIFF_HPS * 2 * DIFF_TQ), jnp.float32),
                        pltpu.VMEM((DIFF_HPS * 2 * DIFF_TQ // CHAIN, DIFF_TK, CHAIN), jnp.float32),
                        pltpu.VMEM((DIFF_HPS * 2 * DIFF_TQ // CHAIN, DIFF_TK, CHAIN), jnp.float32)],
        compiler_params=_params("parallel", "parallel", "parallel"),
        name="diff_attention",
    )(qbt, kb, vbt, lq1, lk1, lq2, lk2, subln)


def _merge_kernel(x_ref, g_ref, wm_ref, bm_ref, oa_ref, ob_ref, oc_ref, wa_ref, wb_ref, wc_ref, wo_ref, o_ref):
    x = x_ref[...]
    d = x.shape[1]
    hb = _rmsnorm_rows(x, g_ref[...]).astype(MXU_DTYPE)
    mixed = None
    for i, (br_ref, w_ref) in enumerate(((oa_ref, wa_ref), (ob_ref, wb_ref), (oc_ref, wc_ref))):
        cols = slice(i * d, (i + 1) * d)
        gate = jax.nn.sigmoid(_dot(hb, wm_ref[:, cols]) + bm_ref[:, cols])
        proj = _dot(br_ref[...], w_ref[...]) if br_ref is oc_ref else _dot_tn(br_ref[...], w_ref[...])
        term = gate * proj
        mixed = term if mixed is None else mixed + term
    o_ref[...] = x + _dot(mixed.astype(MXU_DTYPE), wo_ref[...])


def _merge(x2, g, wm, bm, oa, ob, oc, wa, wb, wc, wo):
    n, d = x2.shape
    row = lambda w: pl.BlockSpec((TM, w), lambda i: (i, 0))
    return pl.pallas_call(
        _merge_kernel,
        grid=(n // TM,),
        in_specs=[row(d), _const_spec(g.shape), _const_spec(wm.shape), _const_spec(bm.shape),
                  pl.BlockSpec((BRANCH, TM), lambda i: (0, i)), pl.BlockSpec((BRANCH, TM), lambda i: (0, i)), row(BRANCH),
                  _const_spec(wa.shape), _const_spec(wb.shape), _const_spec(wc.shape), _const_spec(wo.shape)],
        out_specs=row(d),
        out_shape=jax.ShapeDtypeStruct((n, d), jnp.float32),
        compiler_params=_params("parallel"),
        name="merge_out",
    )(x2, g, wm, bm, oa, ob, oc, wa, wb, wc, wo)


def _ffn_kernel(final, n_chunks, x_ref, g_ref, w1_ref, w3_ref, w2_ref, gf_ref, o_ref):
    x = x_ref[...]
    hb = _rmsnorm_rows(x, g_ref[...]).astype(MXU_DTYPE)
    cw = w1_ref.shape[1] // n_chunks
    y = x
    for c in range(n_chunks):
        cols = slice(c * cw, (c + 1) * cw)
        act = jax.nn.silu(_dot(hb, w1_ref[:, cols])) * _dot(hb, w3_ref[:, cols])
        y = y + _dot(act.astype(MXU_DTYPE), w2_ref[cols, :])
    if final:
        y = _rmsnorm_rows(y, gf_ref[...])
    o_ref[...] = y


def _ffn(x2, g, w1, w3, w2, gf, final):
    n, d = x2.shape
    dff = w1.shape[1]
    n_chunks = 2 if dff % (2 * LANE) == 0 else 1
    row = pl.BlockSpec((TM, d), lambda i: (i, 0))
    return pl.pallas_call(
        functools.partial(_ffn_kernel, final, n_chunks),
        grid=(n // TM,),
        in_specs=[row, _const_spec(g.shape), _const_spec(w1.shape), _const_spec(w3.shape),
                  _const_spec(w2.shape), _const_spec(gf.shape)],
        out_specs=row,
        out_shape=jax.ShapeDtypeStruct((n, d), jnp.float32),
        compiler_params=_params("parallel"),
        name="ffn",
    )(x2, g, w1, w3, w2, gf)


def _overlap_t(seq):
    nc = seq // CMP_STRIDE
    c_start = jnp.arange(nc)[None, :] * CMP_STRIDE
    s_start = jnp.arange(NSP)[:, None] * SLC_BLOCK
    ov = (c_start < s_start + SLC_BLOCK) & (c_start + CMP_BLOCK > s_start)
    return ov.astype(MXU_DTYPE)


KV_WIDTH = NSA_GROUPS * HEAD_DIM
IN_SPLITS = (("q_a", NSA_HEADS * HEAD_DIM), ("k_cmp", KV_WIDTH), ("v_cmp", KV_WIDTH), ("k_slc", KV_WIDTH),
             ("v_slc", KV_WIDTH), ("k_win", KV_WIDTH), ("v_win", KV_WIDTH), ("g_a", 3 * NSA_HEADS),
             ("q_b", 2 * DIFF_HEADS * HEAD_DIM), ("k_b", 2 * DIFF_HEADS * HEAD_DIM),
             ("v_b", DIFF_HEADS * DIFF_V_DIM), ("uv_c", 2 * SGU_WIDTH))


def _split_in_proj(w):
    cols, off = {}, 0
    for name, width in IN_SPLITS:
        cols[name] = w[:, off:off + width]
        off += width
    assert off == w.shape[1]
    return cols


def _block_onehot(seq):
    return (jnp.arange(seq)[:, None] // SLC_BLOCK == jnp.arange(NSP)[None, :]).astype(MXU_DTYPE)


def kernel(x, positions, attn_norm, w_in, cmp_pos_k, cmp_k_w1, cmp_k_w2, cmp_pos_v, cmp_v_w1, cmp_v_w2, diff_lq1, diff_lk1, diff_lq2, diff_lk2, diff_subln, sgu_norm, sgu_w, sgu_b, w_branch_a, w_branch_b, w_branch_c, w_merge, b_merge, w_out, ffn_norm, w_ffn1, w_ffn3, w_ffn2, final_norm):
    bsz, seq, d = x.shape
    depth = w_in.shape[0]
    n = bsz * seq
    assert seq % NSA_TK == 0 and seq >= WINDOW + NSA_TQ and seq // SLC_BLOCK <= NSP and n % TM == 0
    bf = MXU_DTYPE

    tables = _rope_tables(positions)
    ovt = _overlap_t(seq)
    onehot = _block_onehot(seq)
    causal = jnp.tril(jnp.ones((SGU_CHUNK, SGU_CHUNK), dtype=bool))
    x2 = x.reshape(n, d)
    row = lambda v: v.reshape(1, -1)

    for l in range(depth):
        lambda_init = 0.8 - 0.6 * math.exp(-0.3 * l)
        w = _split_in_proj(w_in[l])
        wp = jnp.concatenate([w["k_cmp"], w["v_cmp"], w["k_slc"], w["k_win"], w["k_b"], w["uv_c"]],
                             axis=1).astype(bf)
        wt = jnp.concatenate([w["q_a"], w["v_slc"], w["v_win"], w["g_a"],
                              jnp.zeros((d, G_ROWS - 3 * NSA_HEADS), x.dtype), w["q_b"], w["v_b"]],
                             axis=1).T.astype(bf)
        sgu_wm = jnp.where(causal[None], sgu_w[l], 0.0).astype(bf)
        sgu_bias = jnp.repeat(sgu_b[l].T, SGU_WIDTH // SGU_GROUPS, axis=1)

        (kcmp, vcmp, kslc, kwin, kb, oc, qat, qart, vslct, vwint, gt, qbt, vbt) = _inproj(
            x2, row(attn_norm[l]), wp, wt, tables, row(sgu_norm[l]), sgu_wm, sgu_bias)

        kc = _compress(kcmp, bsz, cmp_pos_k[l], cmp_k_w1[l], cmp_k_w2[l], transposed=False)
        vct = _compress(vcmp, bsz, cmp_pos_v[l], cmp_v_w1[l], cmp_v_w2[l], transposed=True)

        oa = _nsa(bsz, seq, qat, qart, gt, kc, vct, kslc, vslct, kwin, vwint, ovt, onehot)
        ob = _diff(bsz, seq, lambda_init, qbt, kb, vbt, row(diff_lq1[l]), row(diff_lk1[l]),
                   row(diff_lq2[l]), row(diff_lk2[l]), diff_subln[l].reshape(DIFF_V_DIM, 1))

        x2 = _merge(x2, row(attn_norm[l]), w_merge[l].astype(bf), row(b_merge[l]), oa, ob, oc,
                    w_branch_a[l].astype(bf), w_branch_b[l].astype(bf), w_branch_c[l].astype(bf),
                    w_out[l].astype(bf))
        x2 = _ffn(x2, row(ffn_norm[l]), w_ffn1[l].astype(bf), w_ffn3[l].astype(bf), w_ffn2[l].astype(bf),
                  row(final_norm), final=(l == depth - 1))
    return x2.reshape(bsz, seq, d)
```
